```python
import math
import jax, jax.numpy as jnp
from jax import lax
import numpy as np

D_MODEL = 1024
BATCH = 16
SEQ = 2048
DEPTH = 2

RET_HEADS = 4
RET_DK = 128
RET_DV = 256
RET_CHUNK = 128
ROPE_BASE = 10000.0
GN_EPS = 1e-5
DIFF_HEADS = 8
DIFF_DH = 64
DIFF_QBLOCK = 128
DIFF_EPS = 1e-5
DIL_PATTERNS = ((128, 1), (512, 4), (2048, 16))
N_DIL = len(DIL_PATTERNS)
DIL_HEADS = 4
DIL_DH = 128
DIL_BLOCK = 128
REL_BUCKETS = 32
REL_MAX_DIST = 128
N_BIAS_HEADS = DIFF_HEADS + N_DIL * DIL_HEADS
N_BRANCHES = 3
FFN_HIDDEN = -(-8 * D_MODEL // (3 * 256)) * 256
RMS_EPS = 1e-6
NEG_INF = -1e30
IN_SPLITS = ((RET_HEADS * RET_DK,) * 2 + (RET_HEADS * RET_DV,) * 2 + (DIFF_HEADS * 2 * DIFF_DH,) * 3
             + (DIL_HEADS * DIL_DH,) * (3 * N_DIL) + (N_BRANCHES * D_MODEL,))
IN_WIDTH = sum(IN_SPLITS)

kernel_name = "hybrid_retention_diffattn_dilated_block"


def rmsnorm(x, g, eps=RMS_EPS):
    xf = x.astype(jnp.float32)
    y = xf * lax.rsqrt(jnp.mean(xf * xf, axis=-1, keepdims=True) + eps)
    return (y * g.astype(jnp.float32)).astype(x.dtype)


def rel_bucket(dist):
    n = jnp.maximum(dist, 0)
    exact = REL_BUCKETS // 2
    log_ratio = jnp.log(jnp.maximum(n, exact).astype(jnp.float32) / exact) / math.log(REL_MAX_DIST / exact)
    large = jnp.minimum(exact + (log_ratio * (REL_BUCKETS - exact)).astype(jnp.int32), REL_BUCKETS - 1)
    return jnp.where(n < exact, n, large)


def rotary(x, pos):
    half = x.shape[-1] // 2
    inv = ROPE_BASE ** (-jnp.arange(half, dtype=jnp.float32) / half)
    ang = pos[:, None] * inv[None, :]
    cos, sin = jnp.cos(ang)[:, None, :], jnp.sin(ang)[:, None, :]
    x1, x2 = x[..., :half], x[..., half:]
    return jnp.concatenate([x1 * cos - x2 * sin, x1 * sin + x2 * cos], axis=-1)


def retention(q, k, v, g, gn_gain):
    b, S = q.shape[0], q.shape[1]
    H, dk, dv, c = RET_HEADS, RET_DK, RET_DV, RET_CHUNK
    n = S // c
    pos = jnp.arange(S, dtype=jnp.float32)
    q = rotary(q, pos).reshape(b, n, c, H, dk)
    k = (rotary(k, pos) * (dk ** -0.5)).reshape(b, n, c, H, dk)
    vv = v.reshape(b, n, c, H, dv)
    log_gamma = jnp.log1p(-jnp.exp2(-5.0 - jnp.arange(H, dtype=jnp.float32)))
    i = jnp.arange(c, dtype=jnp.float32)
    rel = i[:, None] - i[None, :]
    decay = jnp.where(rel >= 0, jnp.exp(log_gamma[:, None, None] * jnp.maximum(rel, 0.0)), 0.0)
    scores = jnp.einsum('bnihd,bnjhd->bnhij', q, k) * decay
    y_intra = jnp.einsum('bnhij,bnjhe->bnihe', scores, vv)
    k_to_end = jnp.exp((c - 1.0 - i)[:, None] * log_gamma[None, :])
    kv = jnp.einsum('bnjhd,bnjhe->nbhde', k * k_to_end[:, :, None], vv).astype(jnp.float32)
    chunk_decay = jnp.exp(c * log_gamma)[:, None, None]

    def step(state, kv_n):
        return state * chunk_decay + kv_n, state

    _, state_prev = lax.scan(step, jnp.zeros(kv.shape[1:], jnp.float32), kv)
    q_from_start = jnp.exp((i + 1.0)[:, None] * log_gamma[None, :])
    y_cross = jnp.einsum('bnihd,nbhde->bnihe', q * q_from_start[:, :, None], state_prev)
    y = (y_intra + y_cross).reshape(b, S, H, dv).astype(jnp.float32)
    mu = jnp.mean(y, axis=-1, keepdims=True)
    var = jnp.mean(jnp.square(y - mu), axis=-1, keepdims=True)
    y = ((y - mu) * lax.rsqrt(var + GN_EPS)).reshape(b, S, H * dv) * gn_gain.astype(jnp.float32)
    return (jax.nn.silu(g.astype(jnp.float32)) * y).astype(v.dtype)


def diff_attention(q, k, v, lam_params, lam_init, subln_gain, bias_table):
    b, S, H, _, dh = q.shape
    QB = DIFF_QBLOCK
    nb = S // QB
    lp = lam_params.astype(jnp.float32)
    lam = jnp.exp(jnp.sum(lp[0] * lp[1])) - jnp.exp(jnp.sum(lp[2] * lp[3])) + lam_init
    q_blocks = jnp.moveaxis(q.reshape(b, nb, QB, H, 2, dh), 1, 0)
    k_pos = jnp.arange(S)
    vf = v.astype(jnp.float32)

    def one_block(args):
        qblk, blk = args
        dist = (blk * QB + jnp.arange(QB))[:, None] - k_pos[None, :]
        bias = jnp.moveaxis(bias_table[rel_bucket(dist)].astype(jnp.float32), -1, 0)
        bias = jnp.where(dist >= 0, bias, NEG_INF)
        s = jnp.einsum('bqhmd,bkhmd->bhmqk', qblk, k).astype(jnp.float32) * (dh ** -0.5) + bias[:, None]
        p = jax.nn.softmax(s, axis=-1)
        a = p[:, :, 0] - lam * p[:, :, 1]
        return jnp.einsum('bhqk,bkhe->bqhe', a, vf)

    o = lax.map(one_block, (q_blocks, jnp.arange(nb)))
    o = jnp.moveaxis(o, 0, 1).reshape(b, S, H, 2 * dh)
    o = rmsnorm(o, subln_gain, DIFF_EPS) * (1.0 - lam_init)
    return o.reshape(b, S, H * 2 * dh).astype(v.dtype)


def dilated_group(q, k, v, window, dilation, bias_table):
    b, S, H, dh = q.shape
    L = DIL_BLOCK
    n_sub = S // dilation
    nb = -(-n_sub // L)
    n_pad = nb * L

    def to_sub(t):
        t = jnp.swapaxes(t.reshape(b, n_sub, dilation, H, dh), 1, 2)
        return jnp.pad(t, ((0, 0), (0, 0), (0, n_pad - n_sub), (0, 0), (0, 0)))

    def kv_band(t):
        prev = jnp.pad(t, ((0, 0), (0, 0), (L, 0), (0, 0), (0, 0)))[:, :, :n_pad]
        return jnp.concatenate([prev.reshape(b, dilation, nb, L, H, dh),
                                t.reshape(b, dilation, nb, L, H, dh)], axis=3)

    qb = to_sub(q).reshape(b, dilation, nb, L, H, dh)
    kb = kv_band(to_sub(k))
    vb = kv_band(to_sub(v)).astype(jnp.float32)
    qi = jnp.arange(L)
    kj = jnp.arange(2 * L)
    m = qi[:, None] + L - kj[None, :]
    key_sub = (jnp.arange(nb) * L)[:, None] - L + kj[None, :]
    mask = ((m >= 0) & (m <= window // dilation))[None] & (key_sub >= 0)[:, None, :]
    bias = jnp.transpose(bias_table[rel_bucket(m * dilation)], (2, 0, 1)).astype(jnp.float32)
    s = jnp.einsum('bdnqhe,bdnkhe->bdnhqk', qb, kb).astype(jnp.float32) * (dh ** -0.5) + bias
    s = jnp.where(mask[:, None], s, NEG_INF)
    mx = jnp.max(s, axis=-1, keepdims=True)
    p = jnp.exp(s - mx)
    den = jnp.sum(p, axis=-1)
    num = jnp.einsum('bdnhqk,bdnkhe->bdnqhe', p, vb)

    def from_sub(t):
        t = t.reshape((b, dilation, n_pad) + t.shape[4:])[:, :, :n_sub]
        return jnp.swapaxes(t, 1, 2).reshape((b, S) + t.shape[3:])

    return (from_sub(num), from_sub(jnp.swapaxes(mx[..., 0], 3, 4)), from_sub(jnp.swapaxes(den, 3, 4)))


def dilated_attention(parts, bias_table):
    b, S = parts[0].shape[0], parts[0].shape[1]
    nums, mxs, dens = [], [], []
    for gi, (window, dilation) in enumerate(DIL_PATTERNS):
        q, k, v = [t.reshape(b, S, DIL_HEADS, DIL_DH) for t in parts[3 * gi:3 * gi + 3]]
        num, mx, den = dilated_group(q, k, v, window, dilation,
                                     bias_table[:, gi * DIL_HEADS:(gi + 1) * DIL_HEADS])
        nums.append(num)
        mxs.append(mx)
        dens.append(den)
    num, mx, den = jnp.stack(nums), jnp.stack(mxs), jnp.stack(dens)
    wgt = jnp.exp(mx - jnp.max(mx, axis=0, keepdims=True))
    o = jnp.sum(wgt[..., None] * num, axis=0) / jnp.sum(wgt * den, axis=0)[..., None]
    return o.reshape(b, S, DIL_HEADS * DIL_DH).astype(parts[0].dtype)


def setup_inputs(seed: int = 0) -> dict:
    key = jax.random.key(seed)
    ks = jax.random.split(key, 16)

    def nrm(k, shape, scale):
        return jax.random.normal(k, shape, jnp.float32) * scale

    return {
        "x": nrm(ks[0], (BATCH, SEQ, D_MODEL), 1.0),
        "w_in": nrm(ks[1], (DEPTH, D_MODEL, IN_WIDTH), D_MODEL ** -0.5),
        "w_branch_ret": nrm(ks[2], (DEPTH, RET_HEADS * RET_DV, D_MODEL), (RET_HEADS * RET_DV) ** -0.5),
        "w_branch_diff": nrm(ks[3], (DEPTH, DIFF_HEADS * 2 * DIFF_DH, D_MODEL), (DIFF_HEADS * 2 * DIFF_DH) ** -0.5),
        "w_branch_dil": nrm(ks[4], (DEPTH, DIL_HEADS * DIL_DH, D_MODEL), (DIL_HEADS * DIL_DH) ** -0.5),
        "w_out": nrm(ks[5], (DEPTH, D_MODEL, D_MODEL), D_MODEL ** -0.5),
        "norm_mix": 1.0 + nrm(ks[6], (DEPTH, D_MODEL), 0.02),
        "norm_ffn": 1.0 + nrm(ks[7], (DEPTH, D_MODEL), 0.02),
        "ret_gn_gain": 1.0 + nrm(ks[8], (DEPTH, RET_HEADS * RET_DV), 0.02),
        "diff_lambda": nrm(ks[9], (DEPTH, 4, DIFF_DH), 0.1),
        "diff_subln_gain": 1.0 + nrm(ks[10], (DEPTH, 2 * DIFF_DH), 0.02),
        "rel_bias": nrm(ks[11], (REL_BUCKETS, N_BIAS_HEADS), 0.2),
        "w_ffn_gate": nrm(ks[12], (DEPTH, D_MODEL, FFN_HIDDEN), D_MODEL ** -0.5),
        "w_ffn_up": nrm(ks[13], (DEPTH, D_MODEL, FFN_HIDDEN), D_MODEL ** -0.5),
        "w_ffn_down": nrm(ks[14], (DEPTH, FFN_HIDDEN, D_MODEL), FFN_HIDDEN ** -0.5),
        "norm_final": 1.0 + nrm(ks[15], (D_MODEL,), 0.02),
    }


def reference(x, w_in, w_branch_ret, w_branch_diff, w_branch_dil, w_out, norm_mix, norm_ffn,
              ret_gn_gain, diff_lambda, diff_subln_gain, rel_bias, w_ffn_gate, w_ffn_up, w_ffn_down,
              norm_final):
    b, S, _ = x.shape
    cuts = np.cumsum(IN_SPLITS)[:-1].tolist()
    for l in range(DEPTH):
        h = rmsnorm(x, norm_mix[l])
        parts = jnp.split(h @ w_in[l], cuts, axis=-1)
        rq, rk, rv, rg = parts[0:4]
        dq, dkk, dvv = parts[4:7]
        dil_parts = parts[7:7 + 3 * N_DIL]
        gates = jax.nn.sigmoid(parts[-1].astype(jnp.float32)).reshape(b, S, N_BRANCHES, D_MODEL).astype(x.dtype)
        y_ret = retention(rq.reshape(b, S, RET_HEADS, RET_DK), rk.reshape(b, S, RET_HEADS, RET_DK),
                          rv, rg, ret_gn_gain[l])
        lam_init = 0.8 - 0.6 * math.exp(-0.3 * l)
        y_diff = diff_attention(dq.reshape(b, S, DIFF_HEADS, 2, DIFF_DH), dkk.reshape(b, S, DIFF_HEADS, 2, DIFF_DH),
                                dvv.reshape(b, S, DIFF_HEADS, 2 * DIFF_DH), diff_lambda[l], lam_init,
                                diff_subln_gain[l], rel_bias[:, :DIFF_HEADS])
        y_dil = dilated_attention(dil_parts, rel_bias[:, DIFF_HEADS:])
        merged = (gates[:, :, 0] * (y_ret @ w_branch_ret[l])
                  + gates[:, :, 1] * (y_diff @ w_branch_diff[l])
                  + gates[:, :, 2] * (y_dil @ w_branch_dil[l]))
        x = x + merged @ w_out[l]
        h = rmsnorm(x, norm_ffn[l])
        x = x + (jax.nn.silu(h @ w_ffn_gate[l]) * (h @ w_ffn_up[l])) @ w_ffn_down[l]
    return rmsnorm(x, norm_final)
```

```python
import functools
import math

import numpy as np
import jax
import jax.numpy as jnp
from jax import lax
from jax.experimental import pallas as pl
from jax.experimental.pallas import tpu as pltpu

F32 = jnp.float32
BF16 = jnp.bfloat16

D_MODEL = 1024
DEPTH = 2
RET_HEADS, RET_DK, RET_DV, RET_CHUNK = 4, 128, 256, 128
ROPE_BASE = 10000.0
GN_EPS = 1e-5
DIFF_HEADS, DIFF_DH = 8, 64
DIFF_EPS = 1e-5
DIL_PATTERNS = ((128, 1), (512, 4), (2048, 16))
N_DIL = len(DIL_PATTERNS)
DIL_HEADS, DIL_DH, DIL_BLOCK = 4, 128, 128
REL_BUCKETS, REL_MAX_DIST = 32, 128
N_BIAS_HEADS = DIFF_HEADS + N_DIL * DIL_HEADS
FFN_HIDDEN = -(-8 * D_MODEL // (3 * 256)) * 256
RMS_EPS = 1e-6
NEG_INF = -1e30

_RET_W = 2 * RET_HEADS * RET_DK + 2 * RET_HEADS * RET_DV
_DIFF_W = 3 * DIFF_HEADS * 2 * DIFF_DH
_DIL_W = 3 * DIL_HEADS * DIL_DH
_GATE_W = 3 * D_MODEL
_OFF_DIFF = _RET_W
_OFF_DIL = _RET_W + _DIFF_W
_OFF_GATE = _OFF_DIL + N_DIL * _DIL_W
_MAIN_W = _GATE_W + _RET_W + _DIFF_W + _DIL_W
_M_RET = _GATE_W
_M_DIFF = _GATE_W + _RET_W
_M_DIL0 = _GATE_W + _RET_W + _DIFF_W

LANE = 128
VMEM_LIMIT = 56 * 1024 * 1024


def _cparams(sem):
    return pltpu.CompilerParams(dimension_semantics=sem, vmem_limit_bytes=VMEM_LIMIT)


def _rms(x, g, eps):
    return x * lax.rsqrt(jnp.mean(x * x, axis=-1, keepdims=True) + eps) * g


def _dot(a, b):
    return jnp.dot(a, b, preferred_element_type=F32)


def _dot_nt(a, b):
    return lax.dot_general(a, b, (((1,), (1,)), ((), ())), preferred_element_type=F32)


def _dot_tn(a, b):
    return lax.dot_general(a, b, (((0,), (0,)), ((), ())), preferred_element_type=F32)


def _inproj_kernel(x_ref, g_ref, w_ref, o_ref, h_ref):
    @pl.when(pl.program_id(1) == 0)
    def _():
        h_ref[...] = _rms(x_ref[...], g_ref[...], RMS_EPS).astype(BF16)

    o_ref[...] = _dot(h_ref[...], w_ref[...]).astype(o_ref.dtype)


def _inproj_main(x2, g, w, tm=1024, tn=1536):
    T, D = x2.shape
    N = w.shape[1]
    return pl.pallas_call(
        _inproj_kernel,
        grid=(T // tm, N // tn),
        in_specs=[pl.BlockSpec((tm, D), lambda i, j: (i, 0)),
                  pl.BlockSpec((1, D), lambda i, j: (0, 0)),
                  pl.BlockSpec((D, tn), lambda i, j: (0, j))],
        out_specs=pl.BlockSpec((tm, tn), lambda i, j: (i, j)),
        out_shape=jax.ShapeDtypeStruct((T, N), BF16),
        scratch_shapes=[pltpu.VMEM((tm, D), BF16)],
        compiler_params=_cparams(("parallel", "arbitrary")),
        name="inproj_main",
    )(x2, g, w)


def _inproj_perm_kernel(x_ref, g_ref, w_ref, o_ref, *, k):
    for j in range(k):
        h = _rms(x_ref[0, :, j * D_MODEL:(j + 1) * D_MODEL], g_ref[...], RMS_EPS).astype(BF16)
        o_ref[0, j] = _dot(h, w_ref[...]).astype(o_ref.dtype)


def _inproj_perm(x, g, w, dil, k):
    B, S, D = x.shape
    n_sub = S // dil
    N = w.shape[1]
    xv = x.reshape(B, n_sub, dil * D)
    return pl.pallas_call(
        functools.partial(_inproj_perm_kernel, k=k),
        grid=(B, dil // k),
        in_specs=[pl.BlockSpec((1, n_sub, k * D), lambda b, r: (b, 0, r)),
                  pl.BlockSpec((1, D), lambda b, r: (0, 0)),
                  pl.BlockSpec((D, N), lambda b, r: (0, 0))],
        out_specs=pl.BlockSpec((1, k, n_sub, N), lambda b, r: (b, r, 0, 0)),
        out_shape=jax.ShapeDtypeStruct((B, dil, n_sub, N), BF16),
        compiler_params=_cparams(("parallel", "parallel")),
        name=f"inproj_dil{dil}",
    )(xv, g, w)


def _bucket_patterns():
    i = np.arange(DIL_BLOCK)[:, None]
    j = np.arange(2 * DIL_BLOCK)[None, :]
    m = i + DIL_BLOCK - j

    def bucket(dist):
        n = np.maximum(dist, 0)
        exact = REL_BUCKETS // 2
        log_ratio = (np.log(np.maximum(n, exact).astype(np.float32) / np.float32(exact))
                     / np.float32(math.log(REL_MAX_DIST / exact))).astype(np.float32)
        large = np.minimum(exact + (log_ratio * np.float32(REL_BUCKETS - exact)).astype(np.int32),
                           REL_BUCKETS - 1)
        return np.where(n < exact, n, large).astype(np.int32)

    pats = [np.where(m >= 0, bucket(m), -1)]
    for window, dil in DIL_PATTERNS:
        pats.append(np.where((m >= 0) & (m <= window // dil), bucket(m * dil), -1))
    return np.stack(pats).astype(np.int32)


def _bias_kernel(tbl_ref, pat_ref, tile_ref, far_ref):
    h = pl.program_id(0)
    pat = pat_ref[0]
    tile = jnp.full(pat.shape, NEG_INF, F32)
    for b in range(REL_BUCKETS):
        tile = jnp.where(pat == b, tbl_ref[h, b], tile)
    tile_ref[0] = tile
    far_ref[0] = jnp.full(far_ref.shape[1:], tbl_ref[h, REL_BUCKETS - 1], F32)


def _bias_tiles(rel_bias):
    pats = jnp.asarray(_bucket_patterns())
    tbl = rel_bias.T

    def pat_idx(h):
        return (jnp.where(h < DIFF_HEADS, 0, 1 + (h - DIFF_HEADS) // DIL_HEADS), 0, 0)

    return pl.pallas_call(
        _bias_kernel,
        grid=(N_BIAS_HEADS,),
        in_specs=[pl.BlockSpec(memory_space=pltpu.SMEM),
                  pl.BlockSpec((1, DIL_BLOCK, 2 * DIL_BLOCK), pat_idx)],
        out_specs=[pl.BlockSpec((1, DIL_BLOCK, 2 * DIL_BLOCK), lambda h: (h, 0, 0)),
                   pl.BlockSpec((1, 8, LANE), lambda h: (h, 0, 0))],
        out_shape=[jax.ShapeDtypeStruct((N_BIAS_HEADS, DIL_BLOCK, 2 * DIL_BLOCK), F32),
                   jax.ShapeDtypeStruct((N_BIAS_HEADS, 8, LANE), F32)],
        compiler_params=_cparams(("parallel",)),
        name="bias_tiles",
    )(tbl, pats)


def _ret_kernel(q_ref, k_ref, v_ref, g_ref, cos_ref, sin_ref, dec_ref, kend_ref, qst_ref, cdec_ref,
                gain_ref, o_ref, state_ref):
    c = RET_CHUNK
    state_ref[...] = jnp.zeros_like(state_ref)

    def body(n, carry):
        rows = pl.ds(pl.multiple_of(n * c, c), c)
        cos = cos_ref[rows, :]
        sin = sin_ref[rows, :]
        q = q_ref[0, rows, :].astype(F32)
        k = k_ref[0, rows, :].astype(F32)
        qr = q * cos + pltpu.roll(q, RET_DK // 2, 1) * sin
        kr = (k * cos + pltpu.roll(k, RET_DK // 2, 1) * sin) * (RET_DK ** -0.5)
        v = v_ref[0, rows, :]
        s = _dot_nt(qr.astype(BF16), kr.astype(BF16)) * dec_ref[0]
        y = _dot(s.astype(BF16), v)
        st = state_ref[...]
        y = y + _dot((qr * qst_ref[0]).astype(BF16), st.astype(BF16))
        kv = _dot_tn((kr * kend_ref[0]).astype(BF16), v)
        state_ref[...] = st * cdec_ref[0] + kv
        mu = jnp.mean(y, axis=-1, keepdims=True)
        yc = y - mu
        var = jnp.mean(yc * yc, axis=-1, keepdims=True)
        yn = yc * lax.rsqrt(var + GN_EPS) * gain_ref[...]
        g = g_ref[0, rows, :].astype(F32)
        o_ref[0, rows, :] = (g * jax.nn.sigmoid(g) * yn).astype(o_ref.dtype)
        return carry

    lax.fori_loop(0, q_ref.shape[1] // c, body, 0)


def _ret_tables(S):
    H, dk, c = RET_HEADS, RET_DK, RET_CHUNK
    half = dk // 2
    pos = jnp.arange(S, dtype=F32)
    inv = ROPE_BASE ** (-jnp.arange(half, dtype=F32) / half)
    ang = pos[:, None] * inv[None, :]
    cos, sin = jnp.cos(ang), jnp.sin(ang)
    cos2 = jnp.concatenate([cos, cos], axis=-1)
    sin2 = jnp.concatenate([-sin, sin], axis=-1)
    log_gamma = jnp.log1p(-jnp.exp2(-5.0 - jnp.arange(H, dtype=F32)))
    i = jnp.arange(c, dtype=F32)
    rel = i[:, None] - i[None, :]
    decay = jnp.where(rel >= 0, jnp.exp(log_gamma[:, None, None] * jnp.maximum(rel, 0.0)), 0.0)
    k_to_end = jnp.exp((c - 1.0 - i)[None, :] * log_gamma[:, None])
    q_from_start = jnp.exp((i + 1.0)[None, :] * log_gamma[:, None])
    kend = jnp.broadcast_to(k_to_end[:, :, None], (H, c, dk))
    qst = jnp.broadcast_to(q_from_start[:, :, None], (H, c, dk))
    cdec = jnp.broadcast_to(jnp.exp(c * log_gamma)[:, None, None], (H, 1, RET_DV))
    return cos2, sin2, decay, kend, qst, cdec


def _retention(pm, tables, gain):
    B, S, _ = pm.shape
    H, dk, dv, c = RET_HEADS, RET_DK, RET_DV, RET_CHUNK
    cos2, sin2, decay, kend, qst, cdec = tables
    qb = _M_RET // dk
    kb = qb + H
    vb = (_M_RET + 2 * H * dk) // dv
    gb = vb + H
    const2 = lambda b, h: (0, 0)
    head3 = lambda b, h: (h, 0, 0)
    return pl.pallas_call(
        _ret_kernel,
        grid=(B, H),
        in_specs=[pl.BlockSpec((1, S, dk), lambda b, h: (b, 0, qb + h)),
                  pl.BlockSpec((1, S, dk), lambda b, h: (b, 0, kb + h)),
                  pl.BlockSpec((1, S, dv), lambda b, h: (b, 0, vb + h)),
                  pl.BlockSpec((1, S, dv), lambda b, h: (b, 0, gb + h)),
                  pl.BlockSpec((S, dk), const2),
                  pl.BlockSpec((S, dk), const2),
                  pl.BlockSpec((1, c, c), head3),
                  pl.BlockSpec((1, c, dk), head3),
                  pl.BlockSpec((1, c, dk), head3),
                  pl.BlockSpec((1, 1, dv), head3),
                  pl.BlockSpec((1, dv), lambda b, h: (0, h))],
        out_specs=pl.BlockSpec((1, S, dv), lambda b, h: (b, 0, h)),
        out_shape=jax.ShapeDtypeStruct((B, S, H * dv), BF16),
        scratch_shapes=[pltpu.VMEM((dk, dv), F32)],
        compiler_params=_cparams(("parallel", "parallel")),
        name="retention",
    )(pm, pm, pm, pm, cos2, sin2, decay, kend, qst, cdec, gain)


def _diff_kernel(q_ref, k_ref, v_ref, bias_ref, far_ref, lam_ref, gain_ref, o_ref, *, lam_init):
    S = q_ref.shape[1]
    QB = DIL_BLOCK
    first = lax.broadcasted_iota(jnp.int32, (QB, 2 * DIFF_DH), 1) < DIFF_DH
    lp = lam_ref[...]
    lam = (jnp.exp(jnp.sum(lp[0:1] * lp[1:2], axis=-1, keepdims=True))
           - jnp.exp(jnp.sum(lp[2:3] * lp[3:4], axis=-1, keepdims=True)) + lam_init)
    bias = bias_ref[0] - far_ref[0][0:1, 0:1]
    bias2 = jnp.concatenate([bias, bias], axis=0)
    gain = gain_ref[...] * (1.0 - lam_init)
    zero = jnp.zeros((), BF16)
    for qi in range(S // QB):
        qb = q_ref[0, qi * QB:(qi + 1) * QB, :] * (DIFF_DH ** -0.5)
        qq = jnp.concatenate([jnp.where(first, qb, zero), jnp.where(first, zero, qb)], axis=0)
        lo, hi = max(qi - 1, 0) * QB, (qi + 1) * QB
        sb = _dot_nt(qq, k_ref[0, lo:hi, :]) + (bias2 if qi > 0 else bias2[:, QB:])
        m = jnp.max(sb, axis=-1, keepdims=True)
        if lo > 0:
            sf = _dot_nt(qq, k_ref[0, 0:lo, :])
            m = jnp.maximum(m, jnp.max(sf, axis=-1, keepdims=True))
            ef = jnp.exp(sf - m)
            l = jnp.sum(ef, axis=-1, keepdims=True)
            acc = _dot(ef.astype(BF16), v_ref[0, 0:lo, :])
        eb = jnp.exp(sb - m)
        lb = jnp.sum(eb, axis=-1, keepdims=True)
        accb = _dot(eb.astype(BF16), v_ref[0, lo:hi, :])
        if lo > 0:
            l, acc = l + lb, acc + accb
        else:
            l, acc = lb, accb
        r = acc / l
        o = r[:QB] - lam * r[QB:]
        o = o * lax.rsqrt(jnp.mean(o * o, axis=-1, keepdims=True) + DIFF_EPS) * gain
        o_ref[0, qi * QB:(qi + 1) * QB, :] = o.astype(o_ref.dtype)


def _diff_attention(pm, tiles, far, lam_params, gain, lam_init):
    B, S, _ = pm.shape
    H, w = DIFF_HEADS, 2 * DIFF_DH
    qb = _M_DIFF // w
    kb, vb = qb + H, qb + 2 * H
    return pl.pallas_call(
        functools.partial(_diff_kernel, lam_init=lam_init),
        grid=(B, H),
        in_specs=[pl.BlockSpec((1, S, w), lambda b, h: (b, 0, qb + h)),
                  pl.BlockSpec((1, S, w), lambda b, h: (b, 0, kb + h)),
                  pl.BlockSpec((1, S, w), lambda b, h: (b, 0, vb + h)),
                  pl.BlockSpec((1, DIL_BLOCK, 2 * DIL_BLOCK), lambda b, h: (h, 0, 0)),
                  pl.BlockSpec((1, 8, LANE), lambda b, h: (h, 0, 0)),
                  pl.BlockSpec((4, DIFF_DH), lambda b, h: (0, 0)),
                  pl.BlockSpec((1, w), lambda b, h: (0, 0))],
        out_specs=pl.BlockSpec((1, S, w), lambda b, h: (b, 0, h)),
        out_shape=jax.ShapeDtypeStruct((B, S, H * w), BF16),
        compiler_params=_cparams(("parallel", "parallel")),
        name="diff_attention",
    )(pm, pm, pm, tiles, far, lam_params, gain)


def _dil_unit(q, kb, vb, bias):
    s = _dot_nt(q, kb) * (DIL_DH ** -0.5) + bias
    mx = jnp.max(s, axis=-1, keepdims=True)
    p = jnp.exp(s - mx)
    den = jnp.sum(p, axis=-1, keepdims=True)
    num = _dot(p.astype(BF16), vb)
    return num / den, mx + jnp.log(den)


def _dil_kernel(p_ref, bias_ref, o_ref, lse_ref):
    L, dh, H = DIL_BLOCK, DIL_DH, DIL_HEADS
    n_blocks = p_ref.shape[2] // L

    def q_of(rows, h):
        return p_ref[0, 0, rows, h * dh:(h + 1) * dh]

    def k_of(rows, h):
        return p_ref[0, 0, rows, (H + h) * dh:(H + h + 1) * dh]

    def v_of(rows, h):
        return p_ref[0, 0, rows, (2 * H + h) * dh:(2 * H + h + 1) * dh]

    def emit(rows, h, o, lse):
        o_ref[0, rows, h * dh:(h + 1) * dh] = o
        lse_ref[0, rows, h * dh:(h + 1) * dh] = jnp.broadcast_to(lse, (L, dh))

    first = pl.ds(0, L)
    for h in range(H):
        o, lse = _dil_unit(q_of(first, h), k_of(first, h), v_of(first, h), bias_ref[h][:, L:])
        emit(first, h, o, lse)

    if n_blocks > 1:
        def body(n, carry):
            rows = pl.ds(pl.multiple_of(n * L, L), L)
            band = pl.ds(pl.multiple_of((n - 1) * L, L), 2 * L)
            for h in range(H):
                o, lse = _dil_unit(q_of(rows, h), k_of(band, h), v_of(band, h), bias_ref[h])
                emit(rows, h, o, lse)
            return carry

        lax.fori_loop(1, n_blocks, body, 0)


def _dilated_group(pv, col_block, tiles, gi):
    B, dil, n_sub, _ = pv.shape
    L, dh, H = DIL_BLOCK, DIL_DH, DIL_HEADS
    h0 = DIFF_HEADS + gi * H
    out = jax.ShapeDtypeStruct((B, n_sub, dil * H * dh), F32)
    o, lse = pl.pallas_call(
        _dil_kernel,
        grid=(B, dil),
        in_specs=[pl.BlockSpec((1, 1, n_sub, _DIL_W), lambda b, r: (b, r, 0, col_block)),
                  pl.BlockSpec((H, L, 2 * L), lambda b, r: (h0 // H, 0, 0))],
        out_specs=[pl.BlockSpec((1, n_sub, H * dh), lambda b, r: (b, 0, r)),
                   pl.BlockSpec((1, n_sub, H * dh), lambda b, r: (b, 0, r))],
        out_shape=[out, out],
        compiler_params=_cparams(("parallel", "parallel")),
        name=f"dilated_g{gi}",
    )(pv, tiles)
    S = n_sub * dil
    return o.reshape(B * S, H * dh), lse.reshape(B * S, H * dh)


def _merge_kernel(gate_ref, yr_ref, yd_ref, o0_ref, o1_ref, o2_ref, l0_ref, l1_ref, l2_ref, x_ref,
                  wr_ref, wd_ref, wl_ref, wo_ref, out_ref):
    l0, l1, l2 = l0_ref[...], l1_ref[...], l2_ref[...]
    mx = jnp.maximum(jnp.maximum(l0, l1), l2)
    e0, e1, e2 = jnp.exp(l0 - mx), jnp.exp(l1 - mx), jnp.exp(l2 - mx)
    y_dil = (e0 * o0_ref[...] + e1 * o1_ref[...] + e2 * o2_ref[...]) / (e0 + e1 + e2)
    D = D_MODEL
    gates = jax.nn.sigmoid(gate_ref[...].astype(F32))
    merged = (gates[:, 0:D] * _dot(yr_ref[...], wr_ref[...])
              + gates[:, D:2 * D] * _dot(yd_ref[...], wd_ref[...])
              + gates[:, 2 * D:3 * D] * _dot(y_dil.astype(BF16), wl_ref[...]))
    out_ref[...] = x_ref[...] + _dot(merged.astype(BF16), wo_ref[...])


def _const_spec(shape):
    return pl.BlockSpec(shape, lambda i: (0,) * len(shape), pipeline_mode=pl.Buffered(1))


def _merge(pm2, y_ret, y_diff, dil_outs, x2, wr, wd, wl, wo, tm=256):
    T, D = x2.shape
    dw = DIL_HEADS * DIL_DH
    row = lambda w: pl.BlockSpec((tm, w), lambda i: (i, 0))
    (o0, l0), (o1, l1), (o2, l2) = dil_outs
    return pl.pallas_call(
        _merge_kernel,
        grid=(T // tm,),
        in_specs=[row(_GATE_W), row(D), row(D), row(dw), row(dw), row(dw), row(dw), row(dw), row(dw),
                  row(D), _const_spec(wr.shape), _const_spec(wd.shape), _const_spec(wl.shape),
                  _const_spec(wo.shape)],
        out_specs=row(D),
        out_shape=jax.ShapeDtypeStruct((T, D), F32),
        compiler_params=_cparams(("parallel",)),
        name="merge_out",
    )(pm2, y_ret, y_diff, o0, o1, o2, l0, l1, l2, x2, wr, wd, wl, wo)


def _ffn_kernel(x_ref, g_ref, wg_ref, wu_ref, wd_ref, gf_ref, o_ref, *, final):
    x = x_ref[...]
    h = _rms(x, g_ref[...], RMS_EPS).astype(BF16)
    a = _dot(h, wg_ref[...])
    u = _dot(h, wu_ref[...])
    z = (a * jax.nn.sigmoid(a) * u).astype(BF16)
    y = x + _dot(z, wd_ref[...])
    if final:
        y = _rms(y, gf_ref[...], RMS_EPS)
    o_ref[...] = y


def _ffn(x2, g, wg, wu, wd, g_final, final, tm=512):
    T, D = x2.shape
    row = pl.BlockSpec((tm, D), lambda i: (i, 0))
    return pl.pallas_call(
        functools.partial(_ffn_kernel, final=final),
        grid=(T // tm,),
        in_specs=[row, _const_spec((1, D)), _const_spec(wg.shape), _const_spec(wu.shape),
                  _const_spec(wd.shape), _const_spec((1, D))],
        out_specs=row,
        out_shape=jax.ShapeDtypeStruct((T, D), F32),
        compiler_params=_cparams(("parallel",)),
        name="ffn_final" if final else "ffn",
    )(x2, g, wg, wu, wd, g_final)


def kernel(x, w_in, w_branch_ret, w_branch_diff, w_branch_dil, w_out, norm_mix, norm_ffn, ret_gn_gain, diff_lambda, diff_subln_gain, rel_bias, w_ffn_gate, w_ffn_up, w_ffn_down, norm_final):
    B, S, D = x.shape
    T = B * S
    tiles, far = _bias_tiles(rel_bias)
    ret_tables = _ret_tables(S)
    x2 = x.reshape(T, D)
    for l in range(DEPTH):
        w = w_in[l]
        w_main = jnp.concatenate([w[:, _OFF_GATE:], w[:, :_OFF_DIL + _DIL_W]], axis=1).astype(BF16)
        g_mix = norm_mix[l].reshape(1, D)
        pm2 = _inproj_main(x2, g_mix, w_main)
        pm = pm2.reshape(B, S, _MAIN_W)
        x3 = x2.reshape(B, S, D)

        y_ret = _retention(pm, ret_tables, ret_gn_gain[l].reshape(1, -1))
        lam_init = 0.8 - 0.6 * math.exp(-0.3 * l)
        y_diff = _diff_attention(pm, tiles, far, diff_lambda[l], diff_subln_gain[l].reshape(1, -1),
                                 lam_init)
        dil_outs = [_dilated_group(pm.reshape(B, 1, S, _MAIN_W), _M_DIL0 // _DIL_W, tiles, 0)]
        for gi in range(1, N_DIL):
            dil = DIL_PATTERNS[gi][1]
            off = _OFF_DIL + gi * _DIL_W
            pg = _inproj_perm(x3, g_mix, w[:, off:off + _DIL_W].astype(BF16), dil, k=4)
            dil_outs.append(_dilated_group(pg, 0, tiles, gi))

        x2 = _merge(pm2, y_ret.reshape(T, -1), y_diff.reshape(T, -1), dil_outs, x2,
                    w_branch_ret[l].astype(BF16), w_branch_diff[l].astype(BF16),
                    w_branch_dil[l].astype(BF16), w_out[l].astype(BF16))
        x2 = _ffn(x2, norm_ffn[l].reshape(1, D), w_ffn_gate[l].astype(BF16), w_ffn_up[l].astype(BF16),
                  w_ffn_down[l].astype(BF16), norm_final.reshape(1, D), final=(l == DEPTH - 1))
    return x2.reshape(B, S, D)
```

```python
import functools
import math

import numpy as np
import jax
import jax.numpy as jnp
from jax import lax
from jax.experimental import pallas as pl
from jax.experimental.pallas import tpu as pltpu

F32 = jnp.float32
BF16 = jnp.bfloat16

D_MODEL = 1024
DEPTH = 2
RET_HEADS, RET_DK, RET_DV, RET_CHUNK = 4, 128, 256, 128
ROPE_BASE = 10000.0
GN_EPS = 1e-5
DIFF_HEADS, DIFF_DH = 8, 64
DIFF_EPS = 1e-5
DIL_PATTERNS = ((128, 1), (512, 4), (2048, 16))
N_DIL = len(DIL_PATTERNS)
DIL_HEADS, DIL_DH, DIL_BLOCK = 4, 128, 128
REL_BUCKETS, REL_MAX_DIST = 32, 128
N_BIAS_HEADS = DIFF_HEADS + N_DIL * DIL_HEADS
FFN_HIDDEN = -(-8 * D_MODEL // (3 * 256)) * 256
RMS_EPS = 1e-6
NEG_INF = -1e30

_RET_W = 2 * RET_HEADS * RET_DK + 2 * RET_HEADS * RET_DV
_DIFF_W = 3 * DIFF_HEADS * 2 * DIFF_DH
_DIL_W = 3 * DIL_HEADS * DIL_DH
_GATE_W = 3 * D_MODEL
_OFF_DIFF = _RET_W
_OFF_DIL = _RET_W + _DIFF_W
_OFF_GATE = _OFF_DIL + N_DIL * _DIL_W
_MAIN_W = _GATE_W + _RET_W + _DIFF_W + _DIL_W
_M_RET = _GATE_W
_M_DIFF = _GATE_W + _RET_W
_M_DIL0 = _GATE_W + _RET_W + _DIFF_W

_LOG2E = math.log2(math.e)
_DIFF_AHEAD = 3

LANE = 128
VMEM_LIMIT = 56 * 1024 * 1024


def _cparams(sem):
    return pltpu.CompilerParams(dimension_semantics=sem, vmem_limit_bytes=VMEM_LIMIT)


def _rms(x, g, eps):
    return x * lax.rsqrt(jnp.mean(x * x, axis=-1, keepdims=True) + eps) * g


def _dot(a, b):
    return jnp.dot(a, b, preferred_element_type=F32)


def _dot_nt(a, b):
    return lax.dot_general(a, b, (((1,), (1,)), ((), ())), preferred_element_type=F32)


def _dot_tn(a, b):
    return lax.dot_general(a, b, (((0,), (0,)), ((), ())), preferred_element_type=F32)


def _exp_bf16(x):
    return jnp.exp2((x * _LOG2E).astype(BF16))


def _inproj_kernel(x_ref, g_ref, w_ref, o_ref, h_ref):
    @pl.when(pl.program_id(1) == 0)
    def _():
        h_ref[...] = _rms(x_ref[...], g_ref[...], RMS_EPS).astype(BF16)

    o_ref[...] = _dot(h_ref[...], w_ref[...]).astype(o_ref.dtype)


def _inproj_main(x2, g, w, tm=1024, tn=1536):
    T, D = x2.shape
    N = w.shape[1]
    return pl.pallas_call(
        _inproj_kernel,
        grid=(T // tm, N // tn),
        in_specs=[pl.BlockSpec((tm, D), lambda i, j: (i, 0)),
                  pl.BlockSpec((1, D), lambda i, j: (0, 0)),
                  pl.BlockSpec((D, tn), lambda i, j: (0, j))],
        out_specs=pl.BlockSpec((tm, tn), lambda i, j: (i, j)),
        out_shape=jax.ShapeDtypeStruct((T, N), BF16),
        scratch_shapes=[pltpu.VMEM((tm, D), BF16)],
        compiler_params=_cparams(("parallel", "arbitrary")),
        name="inproj_main",
    )(x2, g, w)


def _inproj_perm_kernel(x_ref, g_ref, w_ref, o_ref, *, k):
    for j in range(k):
        h = _rms(x_ref[0, :, j * D_MODEL:(j + 1) * D_MODEL], g_ref[...], RMS_EPS).astype(BF16)
        o_ref[0, j] = _dot(h, w_ref[...]).astype(o_ref.dtype)


def _inproj_perm(x, g, w, dil, k):
    B, S, D = x.shape
    n_sub = S // dil
    N = w.shape[1]
    xv = x.reshape(B, n_sub, dil * D)
    return pl.pallas_call(
        functools.partial(_inproj_perm_kernel, k=k),
        grid=(B, dil // k),
        in_specs=[pl.BlockSpec((1, n_sub, k * D), lambda b, r: (b, 0, r)),
                  pl.BlockSpec((1, D), lambda b, r: (0, 0)),
                  pl.BlockSpec((D, N), lambda b, r: (0, 0))],
        out_specs=pl.BlockSpec((1, k, n_sub, N), lambda b, r: (b, r, 0, 0)),
        out_shape=jax.ShapeDtypeStruct((B, dil, n_sub, N), BF16),
        compiler_params=_cparams(("parallel", "parallel")),
        name=f"inproj_dil{dil}",
    )(xv, g, w)


def _bucket_patterns():
    i = np.arange(DIL_BLOCK)[:, None]
    j = np.arange(2 * DIL_BLOCK)[None, :]
    m = i + DIL_BLOCK - j

    def bucket(dist):
        n = np.maximum(dist, 0)
        exact = REL_BUCKETS // 2
        log_ratio = (np.log(np.maximum(n, exact).astype(np.float32) / np.float32(exact))
                     / np.float32(math.log(REL_MAX_DIST / exact))).astype(np.float32)
        large = np.minimum(exact + (log_ratio * np.float32(REL_BUCKETS - exact)).astype(np.int32),
                           REL_BUCKETS - 1)
        return np.where(n < exact, n, large).astype(np.int32)

    pats = [np.where(m >= 0, bucket(m), -1)]
    for window, dil in DIL_PATTERNS:
        pats.append(np.where((m >= 0) & (m <= window // dil), bucket(m * dil), -1))
    return np.stack(pats).astype(np.int32)


def _bias_kernel(tbl_ref, pat_ref, tile_ref, far_ref):
    h = pl.program_id(0)
    pat = pat_ref[0]
    tile = jnp.full(pat.shape, NEG_INF, F32)
    for b in range(REL_BUCKETS):
        tile = jnp.where(pat == b, tbl_ref[h, b], tile)
    tile_ref[0] = tile
    far_ref[0] = jnp.full(far_ref.shape[1:], tbl_ref[h, REL_BUCKETS - 1], F32)


def _bias_tiles(rel_bias):
    pats = jnp.asarray(_bucket_patterns())
    tbl = rel_bias.T

    def pat_idx(h):
        return (jnp.where(h < DIFF_HEADS, 0, 1 + (h - DIFF_HEADS) // DIL_HEADS), 0, 0)

    return pl.pallas_call(
        _bias_kernel,
        grid=(N_BIAS_HEADS,),
        in_specs=[pl.BlockSpec(memory_space=pltpu.SMEM),
                  pl.BlockSpec((1, DIL_BLOCK, 2 * DIL_BLOCK), pat_idx)],
        out_specs=[pl.BlockSpec((1, DIL_BLOCK, 2 * DIL_BLOCK), lambda h: (h, 0, 0)),
                   pl.BlockSpec((1, 8, LANE), lambda h: (h, 0, 0))],
        out_shape=[jax.ShapeDtypeStruct((N_BIAS_HEADS, DIL_BLOCK, 2 * DIL_BLOCK), F32),
                   jax.ShapeDtypeStruct((N_BIAS_HEADS, 8, LANE), F32)],
        compiler_params=_cparams(("parallel",)),
        name="bias_tiles",
    )(tbl, pats)


def _ret_kernel(q_ref, k_ref, v_ref, g_ref, cos_ref, sin_ref, dec_ref, kend_ref, qst_ref, cdec_ref,
                gain_ref, o_ref):
    c = RET_CHUNK
    n_chunks = q_ref.shape[1] // c
    scores, q_cross, kvs = [], [], []
    for n in range(n_chunks):
        rows = pl.ds(n * c, c)
        cos = cos_ref[rows, :]
        sin = sin_ref[rows, :]
        q = q_ref[0, rows, :].astype(F32)
        k = k_ref[0, rows, :].astype(F32)
        qr = q * cos + pltpu.roll(q, RET_DK // 2, 1) * sin
        kr = (k * cos + pltpu.roll(k, RET_DK // 2, 1) * sin) * (RET_DK ** -0.5)
        scores.append(_dot_nt(qr.astype(BF16), kr.astype(BF16)))
        q_cross.append((qr * qst_ref[0]).astype(BF16))
        if n < n_chunks - 1:
            kvs.append(_dot_tn((kr * kend_ref[0]).astype(BF16), v_ref[0, rows, :]))

    states = [None]
    st = None
    for n in range(n_chunks - 1):
        st = kvs[n] if st is None else st * cdec_ref[0] + kvs[n]
        states.append(st.astype(BF16))

    for n in range(n_chunks):
        rows = pl.ds(n * c, c)
        y = _dot((scores[n] * dec_ref[0]).astype(BF16), v_ref[0, rows, :])
        if states[n] is not None:
            y = y + _dot(q_cross[n], states[n])
        mu = jnp.mean(y, axis=-1, keepdims=True)
        yc = y - mu
        var = jnp.mean(yc * yc, axis=-1, keepdims=True)
        yn = yc * lax.rsqrt(var + GN_EPS) * gain_ref[...]
        g = g_ref[0, rows, :].astype(F32)
        o_ref[0, rows, :] = (g * jax.nn.sigmoid(g) * yn).astype(o_ref.dtype)


def _ret_tables(S):
    H, dk, c = RET_HEADS, RET_DK, RET_CHUNK
    half = dk // 2
    pos = jnp.arange(S, dtype=F32)
    inv = ROPE_BASE ** (-jnp.arange(half, dtype=F32) / half)
    ang = pos[:, None] * inv[None, :]
    cos, sin = jnp.cos(ang), jnp.sin(ang)
    cos2 = jnp.concatenate([cos, cos], axis=-1)
    sin2 = jnp.concatenate([-sin, sin], axis=-1)
    log_gamma = jnp.log1p(-jnp.exp2(-5.0 - jnp.arange(H, dtype=F32)))
    i = jnp.arange(c, dtype=F32)
    rel = i[:, None] - i[None, :]
    decay = jnp.where(rel >= 0, jnp.exp(log_gamma[:, None, None] * jnp.maximum(rel, 0.0)), 0.0)
    k_to_end = jnp.exp((c - 1.0 - i)[None, :] * log_gamma[:, None])
    q_from_start = jnp.exp((i + 1.0)[None, :] * log_gamma[:, None])
    kend = jnp.broadcast_to(k_to_end[:, :, None], (H, c, dk))
    qst = jnp.broadcast_to(q_from_start[:, :, None], (H, c, dk))
    cdec = jnp.broadcast_to(jnp.exp(c * log_gamma)[:, None, None], (H, 1, RET_DV))
    return cos2, sin2, decay, kend, qst, cdec


def _retention(pm, tables, gain):
    B, S, _ = pm.shape
    H, dk, dv, c = RET_HEADS, RET_DK, RET_DV, RET_CHUNK
    cos2, sin2, decay, kend, qst, cdec = tables
    qb = _M_RET // dk
    kb = qb + H
    vb = (_M_RET + 2 * H * dk) // dv
    gb = vb + H
    const2 = lambda b, h: (0, 0)
    head3 = lambda b, h: (h, 0, 0)
    return pl.pallas_call(
        _ret_kernel,
        grid=(B, H),
        in_specs=[pl.BlockSpec((1, S, dk), lambda b, h: (b, 0, qb + h)),
                  pl.BlockSpec((1, S, dk), lambda b, h: (b, 0, kb + h)),
                  pl.BlockSpec((1, S, dv), lambda b, h: (b, 0, vb + h)),
                  pl.BlockSpec((1, S, dv), lambda b, h: (b, 0, gb + h)),
                  pl.BlockSpec((S, dk), const2),
                  pl.BlockSpec((S, dk), const2),
                  pl.BlockSpec((1, c, c), head3),
                  pl.BlockSpec((1, c, dk), head3),
                  pl.BlockSpec((1, c, dk), head3),
                  pl.BlockSpec((1, 1, dv), head3),
                  pl.BlockSpec((1, dv), lambda b, h: (0, h))],
        out_specs=pl.BlockSpec((1, S, dv), lambda b, h: (b, 0, h)),
        out_shape=jax.ShapeDtypeStruct((B, S, H * dv), BF16),
        compiler_params=_cparams(("parallel", "parallel")),
        name="retention",
    )(pm, pm, pm, pm, cos2, sin2, decay, kend, qst, cdec, gain)


def _diff_kernel(q_ref, k_ref, v_ref, bias_ref, far_ref, lam_ref, gain_ref, o_ref, vv_ref, *,
                 lam_init):
    S = q_ref.shape[1]
    QB = DIL_BLOCK
    dv = 2 * DIFF_DH
    vv_ref[:, :dv] = v_ref[0]
    vv_ref[:, dv:] = jnp.ones((S, dv), BF16)
    first = lax.broadcasted_iota(jnp.int32, (QB, 2 * DIFF_DH), 1) < DIFF_DH
    lp = lam_ref[...]
    lam = (jnp.exp(jnp.sum(lp[0:1] * lp[1:2], axis=-1, keepdims=True))
           - jnp.exp(jnp.sum(lp[2:3] * lp[3:4], axis=-1, keepdims=True)) + lam_init)
    bias = bias_ref[0] - far_ref[0][0:1, 0:1]
    bias2 = jnp.concatenate([bias, bias], axis=0)
    gain = gain_ref[...] * (1.0 - lam_init)
    zero = jnp.zeros((), BF16)
    n_q = S // QB

    def bounds(qi):
        return max(qi - 1, 0) * QB, (qi + 1) * QB

    def scores(qi):
        qb = q_ref[0, qi * QB:(qi + 1) * QB, :] * (DIFF_DH ** -0.5)
        qq = jnp.concatenate([jnp.where(first, qb, zero), jnp.where(first, zero, qb)], axis=0)
        lo, hi = bounds(qi)
        sb = _dot_nt(qq, k_ref[0, lo:hi, :]) + (bias2 if qi > 0 else bias2[:, QB:])
        sf = _dot_nt(qq, k_ref[0, 0:lo, :]) if lo > 0 else None
        return sb, sf

    def finish(qi, sb, sf):
        lo, hi = bounds(qi)
        m = jnp.max(sb, axis=-1, keepdims=True)
        if sf is not None:
            m = jnp.maximum(m, jnp.max(sf, axis=-1, keepdims=True))
        acc = _dot(_exp_bf16(sb - m), vv_ref[lo:hi, :])
        if sf is not None:
            acc = acc + _dot(_exp_bf16(sf - m), vv_ref[0:lo, :])
        r = acc[:, :dv] / acc[:, dv:]
        o = r[:QB] - lam * r[QB:]
        o = o * lax.rsqrt(jnp.mean(o * o, axis=-1, keepdims=True) + DIFF_EPS) * gain
        o_ref[0, qi * QB:(qi + 1) * QB, :] = o.astype(o_ref.dtype)

    pending = {}
    for step in range(n_q + _DIFF_AHEAD):
        if step < n_q:
            pending[step] = scores(step)
        if step >= _DIFF_AHEAD:
            finish(step - _DIFF_AHEAD, *pending.pop(step - _DIFF_AHEAD))


def _diff_attention(pm, tiles, far, lam_params, gain, lam_init):
    B, S, _ = pm.shape
    H, w = DIFF_HEADS, 2 * DIFF_DH
    qb = _M_DIFF // w
    kb, vb = qb + H, qb + 2 * H
    return pl.pallas_call(
        functools.partial(_diff_kernel, lam_init=lam_init),
        grid=(B, H),
        in_specs=[pl.BlockSpec((1, S, w), lambda b, h: (b, 0, qb + h)),
                  pl.BlockSpec((1, S, w), lambda b, h: (b, 0, kb + h)),
                  pl.BlockSpec((1, S, w), lambda b, h: (b, 0, vb + h)),
                  pl.BlockSpec((1, DIL_BLOCK, 2 * DIL_BLOCK), lambda b, h: (h, 0, 0)),
                  pl.BlockSpec((1, 8, LANE), lambda b, h: (h, 0, 0)),
                  pl.BlockSpec((4, DIFF_DH), lambda b, h: (0, 0)),
                  pl.BlockSpec((1, w), lambda b, h: (0, 0))],
        out_specs=pl.BlockSpec((1, S, w), lambda b, h: (b, 0, h)),
        out_shape=jax.ShapeDtypeStruct((B, S, H * w), BF16),
        scratch_shapes=[pltpu.VMEM((S, 2 * w), BF16)],
        compiler_params=_cparams(("parallel", "parallel")),
        name="diff_attention",
    )(pm, pm, pm, tiles, far, lam_params, gain)


def _dil_units(qs, kbs, vbs, biases):
    ss = [_dot_nt(q, kb) for q, kb in zip(qs, kbs)]
    ps, mxs, dens = [], [], []
    for s, bias in zip(ss, biases):
        s = s * (DIL_DH ** -0.5) + bias
        mx = jnp.max(s, axis=-1, keepdims=True)
        p = jnp.exp(s - mx)
        dens.append(jnp.sum(p, axis=-1, keepdims=True))
        mxs.append(mx)
        ps.append(p.astype(BF16))
    nums = [_dot(p, vb) for p, vb in zip(ps, vbs)]
    return [(num / den, mx + jnp.log(den)) for num, den, mx in zip(nums, dens, mxs)]


def _dil_kernel(p_ref, bias_ref, o_ref, lse_ref):
    L, dh, H = DIL_BLOCK, DIL_DH, DIL_HEADS
    n_blocks = p_ref.shape[2] // L

    def q_of(rows, h):
        return p_ref[0, 0, rows, h * dh:(h + 1) * dh]

    def k_of(rows, h):
        return p_ref[0, 0, rows, (H + h) * dh:(H + h + 1) * dh]

    def v_of(rows, h):
        return p_ref[0, 0, rows, (2 * H + h) * dh:(2 * H + h + 1) * dh]

    def emit(rows, h, o, lse):
        o_ref[0, rows, h * dh:(h + 1) * dh] = o
        lse_ref[0, rows, h * dh:(h + 1) * dh] = jnp.broadcast_to(lse, (L, dh))

    first = pl.ds(0, L)
    heads = range(H)
    res = _dil_units([q_of(first, h) for h in heads], [k_of(first, h) for h in heads],
                     [v_of(first, h) for h in heads], [bias_ref[h][:, L:] for h in heads])
    for h, (o, lse) in enumerate(res):
        emit(first, h, o, lse)

    if n_blocks > 1:
        def body(n, carry):
            rows = pl.ds(pl.multiple_of(n * L, L), L)
            band = pl.ds(pl.multiple_of((n - 1) * L, L), 2 * L)
            res = _dil_units([q_of(rows, h) for h in heads], [k_of(band, h) for h in heads],
                             [v_of(band, h) for h in heads], [bias_ref[h] for h in heads])
            for h, (o, lse) in enumerate(res):
                emit(rows, h, o, lse)
            return carry

        lax.fori_loop(1, n_blocks, body, 0)


def _dilated_group(pv, col_block, tiles, gi):
    B, dil, n_sub, _ = pv.shape
    L, dh, H = DIL_BLOCK, DIL_DH, DIL_HEADS
    h0 = DIFF_HEADS + gi * H
    out = jax.ShapeDtypeStruct((B, n_sub, dil * H * dh), F32)
    o, lse = pl.pallas_call(
        _dil_kernel,
        grid=(B, dil),
        in_specs=[pl.BlockSpec((1, 1, n_sub, _DIL_W), lambda b, r: (b, r, 0, col_block)),
                  pl.BlockSpec((H, L, 2 * L), lambda b, r: (h0 // H, 0, 0))],
        out_specs=[pl.BlockSpec((1, n_sub, H * dh), lambda b, r: (b, 0, r)),
                   pl.BlockSpec((1, n_sub, H * dh), lambda b, r: (b, 0, r))],
        out_shape=[out, out],
        compiler_params=_cparams(("parallel", "parallel")),
        name=f"dilated_g{gi}",
    )(pv, tiles)
    S = n_sub * dil
    return o.reshape(B * S, H * dh), lse.reshape(B * S, H * dh)


def _merge_kernel(gate_ref, yr_ref, yd_ref, o0_ref, o1_ref, o2_ref, l0_ref, l1_ref, l2_ref, x_ref,
                  wr_ref, wd_ref, wl_ref, wo_ref, out_ref):
    l0, l1, l2 = l0_ref[...], l1_ref[...], l2_ref[...]
    mx = jnp.maximum(jnp.maximum(l0, l1), l2)
    e0, e1, e2 = jnp.exp(l0 - mx), jnp.exp(l1 - mx), jnp.exp(l2 - mx)
    y_dil = (e0 * o0_ref[...] + e1 * o1_ref[...] + e2 * o2_ref[...]) / (e0 + e1 + e2)
    D = D_MODEL
    gates = jax.nn.sigmoid(gate_ref[...].astype(F32))
    merged = (gates[:, 0:D] * _dot(yr_ref[...], wr_ref[...])
              + gates[:, D:2 * D] * _dot(yd_ref[...], wd_ref[...])
              + gates[:, 2 * D:3 * D] * _dot(y_dil.astype(BF16), wl_ref[...]))
    out_ref[...] = x_ref[...] + _dot(merged.astype(BF16), wo_ref[...])


def _const_spec(shape):
    return pl.BlockSpec(shape, lambda i: (0,) * len(shape), pipeline_mode=pl.Buffered(1))


def _merge(pm2, y_ret, y_diff, dil_outs, x2, wr, wd, wl, wo, tm=256):
    T, D = x2.shape
    dw = DIL_HEADS * DIL_DH
    row = lambda w: pl.BlockSpec((tm, w), lambda i: (i, 0))
    (o0, l0), (o1, l1), (o2, l2) = dil_outs
    return pl.pallas_call(
        _merge_kernel,
        grid=(T // tm,),
        in_specs=[row(_GATE_W), row(D), row(D), row(dw), row(dw), row(dw), row(dw), row(dw), row(dw),
                  row(D), _const_spec(wr.shape), _const_spec(wd.shape), _const_spec(wl.shape),
                  _const_spec(wo.shape)],
        out_specs=row(D),
        out_shape=jax.ShapeDtypeStruct((T, D), F32),
        compiler_params=_cparams(("parallel",)),
        name="merge_out",
    )(pm2, y_ret, y_diff, o0, o1, o2, l0, l1, l2, x2, wr, wd, wl, wo)


def _ffn_kernel(x_ref, g_ref, wg_ref, wu_ref, wd_ref, gf_ref, o_ref, *, final):
    x = x_ref[...]
    h = _rms(x, g_ref[...], RMS_EPS).astype(BF16)
    a = _dot(h, wg_ref[...])
    u = _dot(h, wu_ref[...])
    z = (a * jax.nn.sigmoid(a) * u).astype(BF16)
    y = x + _dot(z, wd_ref[...])
    if final:
        y = _rms(y, gf_ref[...], RMS_EPS)
    o_ref[...] = y


def _ffn(x2, g, wg, wu, wd, g_final, final, tm=512):
    T, D = x2.shape
    row = pl.BlockSpec((tm, D), lambda i: (i, 0))
    return pl.pallas_call(
        functools.partial(_ffn_kernel, final=final),
        grid=(T // tm,),
        in_specs=[row, _const_spec((1, D)), _const_spec(wg.shape), _const_spec(wu.shape),
                  _const_spec(wd.shape), _const_spec((1, D))],
        out_specs=row,
        out_shape=jax.ShapeDtypeStruct((T, D), F32),
        compiler_params=_cparams(("parallel",)),
        name="ffn_final" if final else "ffn",
    )(x2, g, wg, wu, wd, g_final)


def kernel(x, w_in, w_branch_ret, w_branch_diff, w_branch_dil, w_out, norm_mix, norm_ffn, ret_gn_gain, diff_lambda, diff_subln_gain, rel_bias, w_ffn_gate, w_ffn_up, w_ffn_down, norm_final):
    B, S, D = x.shape
    T = B * S
    tiles, far = _bias_tiles(rel_bias)
    ret_tables = _ret_tables(S)
    x2 = x.reshape(T, D)
    for l in range(DEPTH):
        w = w_in[l]
        w_main = jnp.concatenate([w[:, _OFF_GATE:], w[:, :_OFF_DIL + _DIL_W]], axis=1).astype(BF16)
        g_mix = norm_mix[l].reshape(1, D)
        pm2 = _inproj_main(x2, g_mix, w_main)
        pm = pm2.reshape(B, S, _MAIN_W)
        x3 = x2.reshape(B, S, D)

        y_ret = _retention(pm, ret_tables, ret_gn_gain[l].reshape(1, -1))
        lam_init = 0.8 - 0.6 * math.exp(-0.3 * l)
        y_diff = _diff_attention(pm, tiles, far, diff_lambda[l], diff_subln_gain[l].reshape(1, -1),
                                 lam_init)
        dil_outs = [_dilated_group(pm.reshape(B, 1, S, _MAIN_W), _M_DIL0 // _DIL_W, tiles, 0)]
        for gi in range(1, N_DIL):
            dil = DIL_PATTERNS[gi][1]
            off = _OFF_DIL + gi * _DIL_W
            pg = _inproj_perm(x3, g_mix, w[:, off:off + _DIL_W].astype(BF16), dil, k=4)
            dil_outs.append(_dilated_group(pg, 0, tiles, gi))

        x2 = _merge(pm2, y_ret.reshape(T, -1), y_diff.reshape(T, -1), dil_outs, x2,
                    w_branch_ret[l].astype(BF16), w_branch_diff[l].astype(BF16),
                    w_branch_dil[l].astype(BF16), w_out[l].astype(BF16))
        x2 = _ffn(x2, norm_ffn[l].reshape(1, D), w_ffn_gate[l].astype(BF16), w_ffn_up[l].astype(BF16),
                  w_ffn_down[l].astype(BF16), norm_final.reshape(1, D), final=(l == DEPTH - 1))
    return x2.reshape(B, S, D)
```

```python
import functools
import math

import numpy as np
import jax
import jax.numpy as jnp
from jax import lax
from jax.experimental import pallas as pl
from jax.experimental.pallas import tpu as pltpu

F32 = jnp.float32
BF16 = jnp.bfloat16

D_MODEL = 1024
DEPTH = 2
RET_HEADS, RET_DK, RET_DV, RET_CHUNK = 4, 128, 256, 128
ROPE_BASE = 10000.0
GN_EPS = 1e-5
DIFF_HEADS, DIFF_DH = 8, 64
DIFF_EPS = 1e-5
DIL_PATTERNS = ((128, 1), (512, 4), (2048, 16))
N_DIL = len(DIL_PATTERNS)
DIL_HEADS, DIL_DH, DIL_BLOCK = 4, 128, 128
REL_BUCKETS, REL_MAX_DIST = 32, 128
N_BIAS_HEADS = DIFF_HEADS + N_DIL * DIL_HEADS
FFN_HIDDEN = -(-8 * D_MODEL // (3 * 256)) * 256
RMS_EPS = 1e-6
NEG_INF = -1e30

_RET_W = 2 * RET_HEADS * RET_DK + 2 * RET_HEADS * RET_DV
_DIFF_W = 3 * DIFF_HEADS * 2 * DIFF_DH
_DIL_W = 3 * DIL_HEADS * DIL_DH
_GATE_W = 3 * D_MODEL
_OFF_DIFF = _RET_W
_OFF_DIL = _RET_W + _DIFF_W
_OFF_GATE = _OFF_DIL + N_DIL * _DIL_W
_MAIN_W = _GATE_W + _RET_W + _DIFF_W + _DIL_W
_M_RET = _GATE_W
_M_DIFF = _GATE_W + _RET_W
_M_DIL0 = _GATE_W + _RET_W + _DIFF_W

_LOG2E = math.log2(math.e)
_DIFF_AHEAD = 3

LANE = 128
VMEM_LIMIT = 56 * 1024 * 1024


def _cparams(sem):
    return pltpu.CompilerParams(dimension_semantics=sem, vmem_limit_bytes=VMEM_LIMIT)


def _rms(x, g, eps):
    return x * lax.rsqrt(jnp.mean(x * x, axis=-1, keepdims=True) + eps) * g


def _dot(a, b):
    return jnp.dot(a, b, preferred_element_type=F32)


def _dot_nt(a, b):
    return lax.dot_general(a, b, (((1,), (1,)), ((), ())), preferred_element_type=F32)


def _dot_tn(a, b):
    return lax.dot_general(a, b, (((0,), (0,)), ((), ())), preferred_element_type=F32)


def _exp_bf16(x):
    return jnp.exp2((x * _LOG2E).astype(BF16))


RES = 16
PERM_ROWS = RES * RES


def _residue_perm():
    p = np.zeros((PERM_ROWS, PERM_ROWS), np.float32)
    a, r = np.meshgrid(np.arange(RES), np.arange(RES), indexing="ij")
    p[(r * RES + a).ravel(), (a * RES + r).ravel()] = 1.0
    return p


def _inproj_kernel(x_ref, g_ref, perm_ref, w_ref, o_ref, o16_ref, h_ref, hp_ref, *, n_main):
    j = pl.program_id(1)
    tm = x_ref.shape[0]
    per_res = tm // RES

    @pl.when(j == 0)
    def _():
        h = _rms(x_ref[...], g_ref[...], RMS_EPS).astype(BF16)
        h_ref[...] = h
        for s in range(tm // PERM_ROWS):
            hs = _dot(perm_ref[...], h[s * PERM_ROWS:(s + 1) * PERM_ROWS]).astype(BF16)
            for r in range(RES):
                hp_ref[pl.ds(r * per_res + s * RES, RES), :] = hs[r * RES:(r + 1) * RES]

    @pl.when(j < n_main)
    def _():
        o_ref[...] = _dot(h_ref[...], w_ref[...]).astype(o_ref.dtype)

    @pl.when(j >= n_main)
    def _():
        res = _dot(hp_ref[...], w_ref[...]).astype(o16_ref.dtype)
        o16_ref[0] = res.reshape(RES, per_res, res.shape[-1])


def _inproj(x2, g, perm, w, B, tm=1024, tn=1536):
    T, D = x2.shape
    S = T // B
    n_main = _MAIN_W // tn
    tiles_per_seq = S // tm
    return pl.pallas_call(
        functools.partial(_inproj_kernel, n_main=n_main),
        grid=(T // tm, w.shape[1] // tn),
        in_specs=[pl.BlockSpec((tm, D), lambda i, j: (i, 0)),
                  pl.BlockSpec((1, D), lambda i, j: (0, 0)),
                  pl.BlockSpec((PERM_ROWS, PERM_ROWS), lambda i, j: (0, 0)),
                  pl.BlockSpec((D, tn), lambda i, j: (0, j))],
        out_specs=[pl.BlockSpec((tm, tn), lambda i, j: (i, jnp.minimum(j, n_main - 1))),
                   pl.BlockSpec((1, RES, tm // RES, tn),
                                lambda i, j: (i // tiles_per_seq, 0, i % tiles_per_seq,
                                              jnp.maximum(j - n_main, 0)))],
        out_shape=[jax.ShapeDtypeStruct((T, _MAIN_W), BF16),
                   jax.ShapeDtypeStruct((B, RES, S // RES, w.shape[1] - _MAIN_W), BF16)],
        scratch_shapes=[pltpu.VMEM((tm, D), BF16), pltpu.VMEM((tm, D), BF16)],
        compiler_params=_cparams(("parallel", "arbitrary")),
        name="inproj",
    )(x2, g, perm, w)


def _bucket_patterns():
    L = DIL_BLOCK
    i = np.arange(L)[:, None]
    j = np.arange(2 * L)[None, :]
    m = i + L - j
    n_str = RES // DIL_PATTERNS[1][1]
    qa = L // n_str
    u_q = 4 * (i % qa) + i // qa
    m_g1 = u_q + L - (4 * (j % (2 * qa)) + j // (2 * qa))
    m_g1_first = np.where(j < L, u_q - (4 * (j % qa) + j // qa), -1)

    def bucket(dist):
        n = np.maximum(dist, 0)
        exact = REL_BUCKETS // 2
        log_ratio = (np.log(np.maximum(n, exact).astype(np.float32) / np.float32(exact))
                     / np.float32(math.log(REL_MAX_DIST / exact))).astype(np.float32)
        large = np.minimum(exact + (log_ratio * np.float32(REL_BUCKETS - exact)).astype(np.int32),
                           REL_BUCKETS - 1)
        return np.where(n < exact, n, large).astype(np.int32)

    def windowed(mm, gi):
        window, dil = DIL_PATTERNS[gi]
        return np.where((mm >= 0) & (mm <= window // dil), bucket(mm * dil), -1)

    pats = [np.where(m >= 0, bucket(m), -1), windowed(m, 0), windowed(m_g1, 1),
            windowed(m_g1_first, 1), windowed(m, 2)]
    return np.stack(pats).astype(np.int32)


_TILE_HEADS = (list(range(DIFF_HEADS + 2 * DIL_HEADS))
               + list(range(DIFF_HEADS + DIL_HEADS, DIFF_HEADS + 3 * DIL_HEADS)))
_TILE_PATS = [0] * DIFF_HEADS + [1] * DIL_HEADS + [2] * DIL_HEADS + [3] * DIL_HEADS + [4] * DIL_HEADS
_TILE_G0, _TILE_G1, _TILE_G1_FIRST, _TILE_G2 = (DIFF_HEADS + k * DIL_HEADS for k in range(4))


def _bias_kernel(head_ref, patid_ref, tbl_ref, pat_ref, tile_ref, far_ref):
    h = head_ref[pl.program_id(0)]
    pat = pat_ref[0]
    tile = jnp.full(pat.shape, NEG_INF, F32)
    for b in range(REL_BUCKETS):
        tile = jnp.where(pat == b, tbl_ref[h, b], tile)
    tile_ref[0] = tile
    far_ref[0] = jnp.full(far_ref.shape[1:], tbl_ref[h, REL_BUCKETS - 1], F32)


def _bias_tiles(rel_bias):
    pats = jnp.asarray(_bucket_patterns())
    tbl = rel_bias.T
    n_tiles = len(_TILE_HEADS)
    shape = (1, DIL_BLOCK, 2 * DIL_BLOCK)
    return pl.pallas_call(
        _bias_kernel,
        grid_spec=pltpu.PrefetchScalarGridSpec(
            num_scalar_prefetch=2,
            grid=(n_tiles,),
            in_specs=[pl.BlockSpec(memory_space=pltpu.SMEM),
                      pl.BlockSpec(shape, lambda t, heads, patids: (patids[t], 0, 0))],
            out_specs=[pl.BlockSpec(shape, lambda t, heads, patids: (t, 0, 0)),
                       pl.BlockSpec((1, 8, LANE), lambda t, heads, patids: (t, 0, 0))]),
        out_shape=[jax.ShapeDtypeStruct((n_tiles,) + shape[1:], F32),
                   jax.ShapeDtypeStruct((n_tiles, 8, LANE), F32)],
        compiler_params=_cparams(("arbitrary",)),
        name="bias_tiles",
    )(jnp.asarray(_TILE_HEADS, jnp.int32), jnp.asarray(_TILE_PATS, jnp.int32), tbl, pats)


def _ret_kernel(q_ref, k_ref, v_ref, g_ref, cos_ref, sin_ref, dec_ref, kend_ref, qst_ref, cdec_ref,
                gain_ref, o_ref):
    c = RET_CHUNK
    n_chunks = q_ref.shape[1] // c
    scores, q_cross, kvs = [], [], []
    for n in range(n_chunks):
        rows = pl.ds(n * c, c)
        cos = cos_ref[rows, :]
        sin = sin_ref[rows, :]
        q = q_ref[0, rows, :].astype(F32)
        k = k_ref[0, rows, :].astype(F32)
        qr = q * cos + pltpu.roll(q, RET_DK // 2, 1) * sin
        kr = (k * cos + pltpu.roll(k, RET_DK // 2, 1) * sin) * (RET_DK ** -0.5)
        scores.append(_dot_nt(qr.astype(BF16), kr.astype(BF16)))
        q_cross.append((qr * qst_ref[0]).astype(BF16))
        if n < n_chunks - 1:
            kvs.append(_dot_tn((kr * kend_ref[0]).astype(BF16), v_ref[0, rows, :]))

    states = [None]
    st = None
    for n in range(n_chunks - 1):
        st = kvs[n] if st is None else st * cdec_ref[0] + kvs[n]
        states.append(st.astype(BF16))

    for n in range(n_chunks):
        rows = pl.ds(n * c, c)
        y = _dot((scores[n] * dec_ref[0]).astype(BF16), v_ref[0, rows, :])
        if states[n] is not None:
            y = y + _dot(q_cross[n], states[n])
        mu = jnp.mean(y, axis=-1, keepdims=True)
        yc = y - mu
        var = jnp.mean(yc * yc, axis=-1, keepdims=True)
        yn = yc * lax.rsqrt(var + GN_EPS) * gain_ref[...]
        g = g_ref[0, rows, :].astype(F32)
        o_ref[0, rows, :] = (g * jax.nn.sigmoid(g) * yn).astype(o_ref.dtype)


def _ret_tables(S):
    H, dk, c = RET_HEADS, RET_DK, RET_CHUNK
    half = dk // 2
    pos = jnp.arange(S, dtype=F32)
    inv = ROPE_BASE ** (-jnp.arange(half, dtype=F32) / half)
    ang = pos[:, None] * inv[None, :]
    cos, sin = jnp.cos(ang), jnp.sin(ang)
    cos2 = jnp.concatenate([cos, cos], axis=-1)
    sin2 = jnp.concatenate([-sin, sin], axis=-1)
    log_gamma = jnp.log1p(-jnp.exp2(-5.0 - jnp.arange(H, dtype=F32)))
    i = jnp.arange(c, dtype=F32)
    rel = i[:, None] - i[None, :]
    decay = jnp.where(rel >= 0, jnp.exp(log_gamma[:, None, None] * jnp.maximum(rel, 0.0)), 0.0)
    k_to_end = jnp.exp((c - 1.0 - i)[None, :] * log_gamma[:, None])
    q_from_start = jnp.exp((i + 1.0)[None, :] * log_gamma[:, None])
    kend = jnp.broadcast_to(k_to_end[:, :, None], (H, c, dk))
    qst = jnp.broadcast_to(q_from_start[:, :, None], (H, c, dk))
    cdec = jnp.broadcast_to(jnp.exp(c * log_gamma)[:, None, None], (H, 1, RET_DV))
    return cos2, sin2, decay, kend, qst, cdec


def _retention(pm, tables, gain):
    B, S, _ = pm.shape
    H, dk, dv, c = RET_HEADS, RET_DK, RET_DV, RET_CHUNK
    cos2, sin2, decay, kend, qst, cdec = tables
    qb = _M_RET // dk
    kb = qb + H
    vb = (_M_RET + 2 * H * dk) // dv
    gb = vb + H
    const2 = lambda b, h: (0, 0)
    head3 = lambda b, h: (h, 0, 0)
    return pl.pallas_call(
        _ret_kernel,
        grid=(B, H),
        in_specs=[pl.BlockSpec((1, S, dk), lambda b, h: (b, 0, qb + h)),
                  pl.BlockSpec((1, S, dk), lambda b, h: (b, 0, kb + h)),
                  pl.BlockSpec((1, S, dv), lambda b, h: (b, 0, vb + h)),
                  pl.BlockSpec((1, S, dv), lambda b, h: (b, 0, gb + h)),
                  pl.BlockSpec((S, dk), const2),
                  pl.BlockSpec((S, dk), const2),
                  pl.BlockSpec((1, c, c), head3),
                  pl.BlockSpec((1, c, dk), head3),
                  pl.BlockSpec((1, c, dk), head3),
                  pl.BlockSpec((1, 1, dv), head3),
                  pl.BlockSpec((1, dv), lambda b, h: (0, h))],
        out_specs=pl.BlockSpec((1, S, dv), lambda b, h: (b, 0, h)),
        out_shape=jax.ShapeDtypeStruct((B, S, H * dv), BF16),
        compiler_params=_cparams(("parallel", "parallel")),
        name="retention",
    )(pm, pm, pm, pm, cos2, sin2, decay, kend, qst, cdec, gain)


def _diff_kernel(q_ref, k_ref, v_ref, bias_ref, far_ref, lam_ref, gain_ref, o_ref, vv_ref, *,
                 lam_init):
    S = q_ref.shape[1]
    QB = DIL_BLOCK
    dv = 2 * DIFF_DH
    vv_ref[:, :dv] = v_ref[0]
    vv_ref[:, dv:] = jnp.ones((S, dv), BF16)
    first = lax.broadcasted_iota(jnp.int32, (QB, 2 * DIFF_DH), 1) < DIFF_DH
    lp = lam_ref[...]
    lam = (jnp.exp(jnp.sum(lp[0:1] * lp[1:2], axis=-1, keepdims=True))
           - jnp.exp(jnp.sum(lp[2:3] * lp[3:4], axis=-1, keepdims=True)) + lam_init)
    bias = bias_ref[0] - far_ref[0][0:1, 0:1]
    bias2 = jnp.concatenate([bias, bias], axis=0)
    gain = gain_ref[...] * (1.0 - lam_init)
    zero = jnp.zeros((), BF16)
    n_q = S // QB

    def bounds(qi):
        return max(qi - 1, 0) * QB, (qi + 1) * QB

    def scores(qi):
        qb = q_ref[0, qi * QB:(qi + 1) * QB, :] * (DIFF_DH ** -0.5)
        qq = jnp.concatenate([jnp.where(first, qb, zero), jnp.where(first, zero, qb)], axis=0)
        lo, hi = bounds(qi)
        sb = _dot_nt(qq, k_ref[0, lo:hi, :]) + (bias2 if qi > 0 else bias2[:, QB:])
        sf = _dot_nt(qq, k_ref[0, 0:lo, :]) if lo > 0 else None
        return sb, sf

    def finish(qi, sb, sf):
        lo, hi = bounds(qi)
        m = jnp.max(sb, axis=-1, keepdims=True)
        if sf is not None:
            m = jnp.maximum(m, jnp.max(sf, axis=-1, keepdims=True))
        acc = _dot(_exp_bf16(sb - m), vv_ref[lo:hi, :])
        if sf is not None:
            acc = acc + _dot(_exp_bf16(sf - m), vv_ref[0:lo, :])
        r = acc[:, :dv] / acc[:, dv:]
        o = r[:QB] - lam * r[QB:]
        o = o * lax.rsqrt(jnp.mean(o * o, axis=-1, keepdims=True) + DIFF_EPS) * gain
        o_ref[0, qi * QB:(qi + 1) * QB, :] = o.astype(o_ref.dtype)

    pending = {}
    for step in range(n_q + _DIFF_AHEAD):
        if step < n_q:
            pending[step] = scores(step)
        if step >= _DIFF_AHEAD:
            finish(step - _DIFF_AHEAD, *pending.pop(step - _DIFF_AHEAD))


def _diff_attention(pm, tiles, far, lam_params, gain, lam_init):
    B, S, _ = pm.shape
    H, w = DIFF_HEADS, 2 * DIFF_DH
    qb = _M_DIFF // w
    kb, vb = qb + H, qb + 2 * H
    return pl.pallas_call(
        functools.partial(_diff_kernel, lam_init=lam_init),
        grid=(B, H),
        in_specs=[pl.BlockSpec((1, S, w), lambda b, h: (b, 0, qb + h)),
                  pl.BlockSpec((1, S, w), lambda b, h: (b, 0, kb + h)),
                  pl.BlockSpec((1, S, w), lambda b, h: (b, 0, vb + h)),
                  pl.BlockSpec((1, DIL_BLOCK, 2 * DIL_BLOCK), lambda b, h: (h, 0, 0)),
                  pl.BlockSpec((1, 8, LANE), lambda b, h: (h, 0, 0)),
                  pl.BlockSpec((4, DIFF_DH), lambda b, h: (0, 0)),
                  pl.BlockSpec((1, w), lambda b, h: (0, 0))],
        out_specs=pl.BlockSpec((1, S, w), lambda b, h: (b, 0, h)),
        out_shape=jax.ShapeDtypeStruct((B, S, H * w), BF16),
        scratch_shapes=[pltpu.VMEM((S, 2 * w), BF16)],
        compiler_params=_cparams(("parallel", "parallel")),
        name="diff_attention",
    )(pm, pm, pm, tiles, far, lam_params, gain)


def _run_pipelined(stages):
    pending = None
    for issue, consume in list(stages) + [(None, None)]:
        issued = issue() if issue is not None else None
        if pending is not None:
            pending[1](pending[0])
        pending = (issued, consume) if consume is not None else None


def _softmax_parts(s, bias):
    s = s * (DIL_DH ** -0.5) + bias
    mx = jnp.max(s, axis=-1, keepdims=True)
    p = jnp.exp(s - mx)
    return p.astype(BF16), mx, jnp.sum(p, axis=-1, keepdims=True)


def _lse_combine(o_a, l_a, o_b, l_b):
    mx = jnp.maximum(l_a, l_b)
    w_a, w_b = jnp.exp(l_a - mx), jnp.exp(l_b - mx)
    den = w_a + w_b
    return (w_a * o_a + w_b * o_b) / den, mx + jnp.log(den)


def _dil_kernel(q0_ref, k0_ref, v0_ref, q1_ref, k1_ref, v1_ref, q2_ref, k2_ref, v2_ref,
                t0_ref, t1_ref, t1f_ref, t2_ref, unperm_ref, y_ref,
                o1_ref, l1_ref, x16_ref, nat_ref):
    L, dh = DIL_BLOCK, DIL_DH
    S = q0_ref.shape[1]
    n_a = S // RES
    dil1 = DIL_PATTERNS[1][1]
    n_str = RES // dil1
    qa = L // n_str
    lane = lax.broadcasted_iota(jnp.int32, (L, dh), 1)
    zeros = jnp.zeros((L, dh), BF16)
    stages = []

    def g1_stage(r4):
        streams = [r4 + dil1 * s for s in range(n_str)]
        n_blocks = n_a // qa

        def gather(ref, lo, hi):
            return jnp.concatenate([ref[0, r, lo:hi, :] for r in streams], axis=0)

        def band(n):
            return (0, qa) if n == 0 else ((n - 1) * qa, (n + 1) * qa)

        def issue():
            return [_dot_nt(gather(q1_ref, n * qa, (n + 1) * qa), gather(k1_ref, *band(n)))
                    for n in range(n_blocks)]

        def consume(ss):
            parts = [_softmax_parts(s, t1f_ref[0][:, :L] if n == 0 else t1_ref[0])
                     for n, s in enumerate(ss)]
            nums = [_dot(p, gather(v1_ref, *band(n))) for n, (p, _, _) in enumerate(parts)]
            for n, (num, (_, mx, den)) in enumerate(zip(nums, parts)):
                o = num / den
                lse = jnp.broadcast_to(mx + jnp.log(den), (L, dh))
                for i, r in enumerate(streams):
                    o1_ref[r, n * qa:(n + 1) * qa, :] = o[i * qa:(i + 1) * qa]
                    l1_ref[r, n * qa:(n + 1) * qa, :] = lse[i * qa:(i + 1) * qa]

        return issue, consume

    stages += [g1_stage(r4) for r4 in range(dil1)]

    def g2_stage(pairs):
        def blockdiag(ref, r):
            return jnp.concatenate([jnp.concatenate([ref[0, r], zeros], axis=1),
                                    jnp.concatenate([zeros, ref[0, r + 1]], axis=1)], axis=0)

        def issue():
            return [_dot_nt(jnp.concatenate([q2_ref[0, r], q2_ref[0, r + 1]], axis=1),
                            blockdiag(k2_ref, r)) for r in pairs]

        def consume(ss):
            bias = t2_ref[0][:, L:]
            parts = [[_softmax_parts(s[:, i * L:(i + 1) * L], bias) for i in range(2)] for s in ss]
            outs = [_dot(jnp.concatenate([pp[0][0], pp[1][0]], axis=1), blockdiag(v2_ref, r))
                    for r, pp in zip(pairs, parts)]
            for r, pp, out in zip(pairs, parts, outs):
                for i in range(2):
                    _, mx, den = pp[i]
                    o2 = out[:, i * L:(i + 1) * L] / den
                    l2 = jnp.broadcast_to(mx + jnp.log(den), (L, dh))
                    o12, l12 = _lse_combine(o1_ref[r + i], l1_ref[r + i], o2, l2)
                    hi = l12.astype(BF16)
                    lo = (l12 - hi.astype(F32)).astype(BF16)
                    x16_ref[r + i] = jnp.concatenate(
                        [o12.astype(BF16), jnp.where(lane < dh // 2, hi, lo)], axis=1)

        return issue, consume

    stages += [g2_stage(range(r0, r0 + RES // 2, 2)) for r0 in (0, RES // 2)]

    def unpermute(_):
        for s in range(S // PERM_ROWS):
            xs = jnp.concatenate([x16_ref[r, s * RES:(s + 1) * RES, :] for r in range(RES)], axis=0)
            nat_ref[s * PERM_ROWS:(s + 1) * PERM_ROWS, :] = _dot(unperm_ref[...], xs)

    stages.append((lambda: None, unpermute))

    def g0_stage(blocks):
        def band(n):
            return (0, L) if n == 0 else ((n - 1) * L, (n + 1) * L)

        def issue():
            return [_dot_nt(q0_ref[0, n * L:(n + 1) * L, :], k0_ref[0, band(n)[0]:band(n)[1], :])
                    for n in blocks]

        def consume(ss):
            parts = [_softmax_parts(s, t0_ref[0][:, L:] if n == 0 else t0_ref[0])
                     for n, s in zip(blocks, ss)]
            nums = [_dot(p, v0_ref[0, band(n)[0]:band(n)[1], :])
                    for n, (p, _, _) in zip(blocks, parts)]
            for n, num, (_, mx, den) in zip(blocks, nums, parts):
                nat = nat_ref[n * L:(n + 1) * L, :]
                l12 = nat[:, dh:dh + 1] + nat[:, dh + dh // 2:dh + dh // 2 + 1]
                y, _ = _lse_combine(num / den, mx + jnp.log(den), nat[:, :dh], l12)
                y_ref[0, n * L:(n + 1) * L, :] = y.astype(y_ref.dtype)

        return issue, consume

    per_stage = 4
    stages += [g0_stage(range(n0, n0 + per_stage)) for n0 in range(0, S // L, per_stage)]
    _run_pipelined(stages)


def _dilated(pm, p16, tiles, unperm):
    B, S, _ = pm.shape
    L, dh, H = DIL_BLOCK, DIL_DH, DIL_HEADS
    n_a = S // RES
    c0 = _M_DIL0 // dh
    nat = lambda c: pl.BlockSpec((1, S, dh), lambda b, h: (b, 0, c + h))
    res = lambda c: pl.BlockSpec((1, RES, n_a, dh), lambda b, h: (b, 0, 0, c + h))
    tile = lambda t: pl.BlockSpec((1, L, 2 * L), lambda b, h: (t + h, 0, 0))
    return pl.pallas_call(
        _dil_kernel,
        grid=(B, H),
        in_specs=[nat(c0), nat(c0 + H), nat(c0 + 2 * H),
                  res(0), res(H), res(2 * H), res(3 * H), res(4 * H), res(5 * H),
                  tile(_TILE_G0), tile(_TILE_G1), tile(_TILE_G1_FIRST), tile(_TILE_G2),
                  pl.BlockSpec((PERM_ROWS, PERM_ROWS), lambda b, h: (0, 0))],
        out_specs=pl.BlockSpec((1, S, dh), lambda b, h: (b, 0, h)),
        out_shape=jax.ShapeDtypeStruct((B, S, H * dh), BF16),
        scratch_shapes=[pltpu.VMEM((RES, n_a, dh), F32), pltpu.VMEM((RES, n_a, dh), F32),
                        pltpu.VMEM((RES, n_a, 2 * dh), BF16), pltpu.VMEM((S, 2 * dh), F32)],
        compiler_params=_cparams(("parallel", "parallel")),
        name="dilated",
    )(pm, pm, pm, p16, p16, p16, p16, p16, p16, tiles, tiles, tiles, tiles, unperm)


def _merge_kernel(gate_ref, yr_ref, yd_ref, yl_ref, x_ref, wr_ref, wd_ref, wl_ref, wo_ref, out_ref):
    D = D_MODEL
    gates = jax.nn.sigmoid(gate_ref[...].astype(F32))
    merged = (gates[:, 0:D] * _dot(yr_ref[...], wr_ref[...])
              + gates[:, D:2 * D] * _dot(yd_ref[...], wd_ref[...])
              + gates[:, 2 * D:3 * D] * _dot(yl_ref[...], wl_ref[...]))
    out_ref[...] = x_ref[...] + _dot(merged.astype(BF16), wo_ref[...])


def _const_spec(shape):
    return pl.BlockSpec(shape, lambda i: (0,) * len(shape), pipeline_mode=pl.Buffered(1))


def _merge(pm2, y_ret, y_diff, y_dil, x2, wr, wd, wl, wo, tm=512):
    T, D = x2.shape
    row = lambda w: pl.BlockSpec((tm, w), lambda i: (i, 0))
    return pl.pallas_call(
        _merge_kernel,
        grid=(T // tm,),
        in_specs=[row(_GATE_W), row(D), row(D), row(y_dil.shape[1]), row(D),
                  _const_spec(wr.shape), _const_spec(wd.shape), _const_spec(wl.shape),
                  _const_spec(wo.shape)],
        out_specs=row(D),
        out_shape=jax.ShapeDtypeStruct((T, D), F32),
        compiler_params=_cparams(("parallel",)),
        name="merge_out",
    )(pm2, y_ret, y_diff, y_dil, x2, wr, wd, wl, wo)


def _ffn_kernel(x_ref, g_ref, wg_ref, wu_ref, wd_ref, gf_ref, o_ref, *, final):
    x = x_ref[...]
    h = _rms(x, g_ref[...], RMS_EPS).astype(BF16)
    a = _dot(h, wg_ref[...])
    u = _dot(h, wu_ref[...])
    z = (a * jax.nn.sigmoid(a) * u).astype(BF16)
    y = x + _dot(z, wd_ref[...])
    if final:
        y = _rms(y, gf_ref[...], RMS_EPS)
    o_ref[...] = y


def _ffn(x2, g, wg, wu, wd, g_final, final, tm=512):
    T, D = x2.shape
    row = pl.BlockSpec((tm, D), lambda i: (i, 0))
    return pl.pallas_call(
        functools.partial(_ffn_kernel, final=final),
        grid=(T // tm,),
        in_specs=[row, _const_spec((1, D)), _const_spec(wg.shape), _const_spec(wu.shape),
                  _const_spec(wd.shape), _const_spec((1, D))],
        out_specs=row,
        out_shape=jax.ShapeDtypeStruct((T, D), F32),
        compiler_params=_cparams(("parallel",)),
        name="ffn_final" if final else "ffn",
    )(x2, g, wg, wu, wd, g_final)


def kernel(x, w_in, w_branch_ret, w_branch_diff, w_branch_dil, w_out, norm_mix, norm_ffn, ret_gn_gain, diff_lambda, diff_subln_gain, rel_bias, w_ffn_gate, w_ffn_up, w_ffn_down, norm_final):
    B, S, D = x.shape
    T = B * S
    tiles, far = _bias_tiles(rel_bias)
    ret_tables = _ret_tables(S)
    perm = _residue_perm()
    perm_fwd, perm_back = jnp.asarray(perm, BF16), jnp.asarray(perm.T, BF16)
    x2 = x.reshape(T, D)
    for l in range(DEPTH):
        w = w_in[l]
        w_all = jnp.concatenate([w[:, _OFF_GATE:], w[:, :_OFF_GATE]], axis=1).astype(BF16)
        pm2, p16 = _inproj(x2, norm_mix[l].reshape(1, D), perm_fwd, w_all, B)
        pm = pm2.reshape(B, S, _MAIN_W)

        y_ret = _retention(pm, ret_tables, ret_gn_gain[l].reshape(1, -1))
        lam_init = 0.8 - 0.6 * math.exp(-0.3 * l)
        y_diff = _diff_attention(pm, tiles, far, diff_lambda[l], diff_subln_gain[l].reshape(1, -1),
                                 lam_init)
        y_dil = _dilated(pm, p16, tiles, perm_back)

        x2 = _merge(pm2, y_ret.reshape(T, -1), y_diff.reshape(T, -1), y_dil.reshape(T, -1), x2,
                    w_branch_ret[l].astype(BF16), w_branch_diff[l].astype(BF16),
                    w_branch_dil[l].astype(BF16), w_out[l].astype(BF16))
        x2 = _ffn(x2, norm_ffn[l].reshape(1, D), w_ffn_gate[l].astype(BF16), w_ffn_up[l].astype(BF16),
                  w_ffn_down[l].astype(BF16), norm_final.reshape(1, D), final=(l == DEPTH - 1))
    return x2.reshape(B, S, D)
```

```python
import functools
import math

import numpy as np
import jax
import jax.numpy as jnp
from jax import lax
from jax.experimental import pallas as pl
from jax.experimental.pallas import tpu as pltpu

F32 = jnp.float32
BF16 = jnp.bfloat16

D_MODEL = 1024
DEPTH = 2
RET_HEADS, RET_DK, RET_DV, RET_CHUNK = 4, 128, 256, 128
ROPE_BASE = 10000.0
GN_EPS = 1e-5
DIFF_HEADS, DIFF_DH = 8, 64
DIFF_EPS = 1e-5
DIL_PATTERNS = ((128, 1), (512, 4), (2048, 16))
N_DIL = len(DIL_PATTERNS)
DIL_HEADS, DIL_DH, DIL_BLOCK = 4, 128, 128
REL_BUCKETS, REL_MAX_DIST = 32, 128
N_BIAS_HEADS = DIFF_HEADS + N_DIL * DIL_HEADS
FFN_HIDDEN = -(-8 * D_MODEL // (3 * 256)) * 256
RMS_EPS = 1e-6
NEG_INF = -1e30

_RET_W = 2 * RET_HEADS * RET_DK + 2 * RET_HEADS * RET_DV
_DIFF_W = 3 * DIFF_HEADS * 2 * DIFF_DH
_DIL_W = 3 * DIL_HEADS * DIL_DH
_GATE_W = 3 * D_MODEL
_OFF_DIFF = _RET_W
_OFF_DIL = _RET_W + _DIFF_W
_OFF_GATE = _OFF_DIL + N_DIL * _DIL_W
_MAIN_W = _GATE_W + _RET_W + _DIFF_W + _DIL_W
_M_RET = _GATE_W
_M_DIFF = _GATE_W + _RET_W
_M_DIL0 = _GATE_W + _RET_W + _DIFF_W

_LOG2E = math.log2(math.e)
_DIL_SCALE = DIL_DH ** -0.5
_DIFF_AHEAD = 3

LANE = 128
VMEM_LIMIT = 56 * 1024 * 1024


def _cparams(sem):
    return pltpu.CompilerParams(dimension_semantics=sem, vmem_limit_bytes=VMEM_LIMIT)


def _rms(x, g, eps):
    return x * lax.rsqrt(jnp.mean(x * x, axis=-1, keepdims=True) + eps) * g


def _dot(a, b):
    return jnp.dot(a, b, preferred_element_type=F32)


def _dot_nt(a, b):
    return lax.dot_general(a, b, (((1,), (1,)), ((), ())), preferred_element_type=F32)


def _dot_tn(a, b):
    return lax.dot_general(a, b, (((0,), (0,)), ((), ())), preferred_element_type=F32)


RES = 16
PERM_ROWS = RES * RES


def _residue_perm():
    p = np.zeros((PERM_ROWS, PERM_ROWS), np.float32)
    a, r = np.meshgrid(np.arange(RES), np.arange(RES), indexing="ij")
    p[(r * RES + a).ravel(), (a * RES + r).ravel()] = 1.0
    return p


def _inproj_kernel(x_ref, g_ref, perm_ref, w_ref, o_ref, o16_ref, h_ref, hp_ref, *, n_main):
    j = pl.program_id(1)
    tm = x_ref.shape[0]
    per_res = tm // RES

    @pl.when(j == 0)
    def _():
        h = _rms(x_ref[...], g_ref[...], RMS_EPS).astype(BF16)
        h_ref[...] = h
        for s in range(tm // PERM_ROWS):
            hs = _dot(perm_ref[...], h[s * PERM_ROWS:(s + 1) * PERM_ROWS]).astype(BF16)
            for r in range(RES):
                hp_ref[pl.ds(r * per_res + s * RES, RES), :] = hs[r * RES:(r + 1) * RES]

    @pl.when(j < n_main)
    def _():
        o_ref[...] = _dot(h_ref[...], w_ref[...]).astype(o_ref.dtype)

    @pl.when(j >= n_main)
    def _():
        res = _dot(hp_ref[...], w_ref[...]).astype(o16_ref.dtype)
        o16_ref[0] = res.reshape(RES, per_res, res.shape[-1])


def _inproj(x2, g, perm, w, l, B, tm=1024, tn=1536):
    T, D = x2.shape
    S = T // B
    n_main = _MAIN_W // tn
    n_tiles = w.shape[2] // tn
    first = _OFF_GATE // tn
    tiles_per_seq = S // tm
    return pl.pallas_call(
        functools.partial(_inproj_kernel, n_main=n_main),
        grid=(T // tm, n_tiles),
        in_specs=[pl.BlockSpec((tm, D), lambda i, j: (i, 0)),
                  pl.BlockSpec((None, 1, D), lambda i, j: (l, 0, 0)),
                  pl.BlockSpec((PERM_ROWS, PERM_ROWS), lambda i, j: (0, 0)),
                  pl.BlockSpec((None, D, tn), lambda i, j: (l, 0, (j + first) % n_tiles))],
        out_specs=[pl.BlockSpec((tm, tn), lambda i, j: (i, jnp.minimum(j, n_main - 1))),
                   pl.BlockSpec((1, RES, tm // RES, tn),
                                lambda i, j: (i // tiles_per_seq, 0, i % tiles_per_seq,
                                              jnp.maximum(j - n_main, 0)))],
        out_shape=[jax.ShapeDtypeStruct((T, _MAIN_W), BF16),
                   jax.ShapeDtypeStruct((B, RES, S // RES, w.shape[2] - _MAIN_W), BF16)],
        scratch_shapes=[pltpu.VMEM((tm, D), BF16), pltpu.VMEM((tm, D), BF16)],
        compiler_params=_cparams(("parallel", "arbitrary")),
        name="inproj",
    )(x2, g, perm, w)


def _bucket_patterns():
    L = DIL_BLOCK
    i = np.arange(L)[:, None]
    j = np.arange(2 * L)[None, :]
    m = i + L - j
    n_str = RES // DIL_PATTERNS[1][1]
    qa = L // n_str
    u_q = 4 * (i % qa) + i // qa
    m_g1 = u_q + L - (4 * (j % (2 * qa)) + j // (2 * qa))
    m_g1_first = np.where(j < L, u_q - (4 * (j % qa) + j // qa), -1)

    def bucket(dist):
        n = np.maximum(dist, 0)
        exact = REL_BUCKETS // 2
        log_ratio = (np.log(np.maximum(n, exact).astype(np.float32) / np.float32(exact))
                     / np.float32(math.log(REL_MAX_DIST / exact))).astype(np.float32)
        large = np.minimum(exact + (log_ratio * np.float32(REL_BUCKETS - exact)).astype(np.int32),
                           REL_BUCKETS - 1)
        return np.where(n < exact, n, large).astype(np.int32)

    def windowed(mm, gi):
        window, dil = DIL_PATTERNS[gi]
        return np.where((mm >= 0) & (mm <= window // dil), bucket(mm * dil), -1)

    pats = [np.where(m >= 0, bucket(m), -1), windowed(m, 0), windowed(m_g1, 1),
            windowed(m_g1_first, 1), windowed(m, 2)]
    return np.stack(pats).astype(np.int32)


_TILE_HEADS = (list(range(DIFF_HEADS + 2 * DIL_HEADS))
               + list(range(DIFF_HEADS + DIL_HEADS, DIFF_HEADS + 3 * DIL_HEADS)))
_TILE_PATS = [0] * DIFF_HEADS + [1] * DIL_HEADS + [2] * DIL_HEADS + [3] * DIL_HEADS + [4] * DIL_HEADS
_TILE_G0, _TILE_G1, _TILE_G1_FIRST, _TILE_G2 = (DIFF_HEADS + k * DIL_HEADS for k in range(4))


def _bias_kernel(head_ref, patid_ref, tbl_ref, pat_ref, tile_ref, far_ref):
    t = pl.program_id(0)
    h = head_ref[t]
    pat = pat_ref[0]
    mult = jnp.where(patid_ref[t] == 0, 1.0, 1.0 / _DIL_SCALE).astype(F32)
    tile = jnp.full(pat.shape, NEG_INF, F32)
    for b in range(REL_BUCKETS):
        tile = jnp.where(pat == b, tbl_ref[h, b] * mult, tile)
    tile_ref[0] = tile
    far_ref[0] = jnp.full(far_ref.shape[1:], tbl_ref[h, REL_BUCKETS - 1], F32)


def _bias_tiles(rel_bias):
    pats = jnp.asarray(_bucket_patterns())
    tbl = rel_bias.T
    n_tiles = len(_TILE_HEADS)
    shape = (1, DIL_BLOCK, 2 * DIL_BLOCK)
    return pl.pallas_call(
        _bias_kernel,
        grid_spec=pltpu.PrefetchScalarGridSpec(
            num_scalar_prefetch=2,
            grid=(n_tiles,),
            in_specs=[pl.BlockSpec(memory_space=pltpu.SMEM),
                      pl.BlockSpec(shape, lambda t, heads, patids: (patids[t], 0, 0))],
            out_specs=[pl.BlockSpec(shape, lambda t, heads, patids: (t, 0, 0)),
                       pl.BlockSpec((1, 8, LANE), lambda t, heads, patids: (t, 0, 0))]),
        out_shape=[jax.ShapeDtypeStruct((n_tiles,) + shape[1:], F32),
                   jax.ShapeDtypeStruct((n_tiles, 8, LANE), F32)],
        compiler_params=_cparams(("arbitrary",)),
        name="bias_tiles",
    )(jnp.asarray(_TILE_HEADS, jnp.int32), jnp.asarray(_TILE_PATS, jnp.int32), tbl, pats)


def _ret_kernel(q_ref, k_ref, v_ref, g_ref, cos_ref, sin_ref, dec_ref, kend_ref, qst_ref, cdec_ref,
                gain_ref, o_ref):
    c = RET_CHUNK
    n_chunks = q_ref.shape[1] // c
    scores, q_cross, kvs = [], [], []
    for n in range(n_chunks):
        rows = pl.ds(n * c, c)
        cos = cos_ref[rows, :]
        sin = sin_ref[rows, :]
        q = q_ref[0, rows, :].astype(F32)
        k = k_ref[0, rows, :].astype(F32)
        qr = q * cos + pltpu.roll(q, RET_DK // 2, 1) * sin
        kr = (k * cos + pltpu.roll(k, RET_DK // 2, 1) * sin) * (RET_DK ** -0.5)
        scores.append(_dot_nt(qr.astype(BF16), kr.astype(BF16)))
        q_cross.append((qr * qst_ref[0]).astype(BF16))
        if n < n_chunks - 1:
            kvs.append(_dot_tn((kr * kend_ref[0]).astype(BF16), v_ref[0, rows, :]))

    states = [None]
    st = None
    for n in range(n_chunks - 1):
        st = kvs[n] if st is None else st * cdec_ref[0] + kvs[n]
        states.append(st.astype(BF16))

    for n in range(n_chunks):
        rows = pl.ds(n * c, c)
        y = _dot((scores[n] * dec_ref[0]).astype(BF16), v_ref[0, rows, :])
        if states[n] is not None:
            y = y + _dot(q_cross[n], states[n])
        mu = jnp.mean(y, axis=-1, keepdims=True)
        yc = y - mu
        var = jnp.mean(yc * yc, axis=-1, keepdims=True)
        yn = yc * lax.rsqrt(var + GN_EPS) * gain_ref[...]
        g = g_ref[0, rows, :].astype(F32)
        o_ref[0, rows, :] = (g * jax.nn.sigmoid(g) * yn).astype(o_ref.dtype)


def _ret_tables(S):
    H, dk, c = RET_HEADS, RET_DK, RET_CHUNK
    half = dk // 2
    pos = np.arange(S, dtype=np.float64)
    inv = ROPE_BASE ** (-np.arange(half, dtype=np.float64) / half)
    ang = pos[:, None] * inv[None, :]
    cos, sin = np.cos(ang), np.sin(ang)
    cos2 = np.concatenate([cos, cos], axis=-1)
    sin2 = np.concatenate([-sin, sin], axis=-1)
    log_gamma = np.log1p(-np.exp2(-5.0 - np.arange(H, dtype=np.float64)))
    i = np.arange(c, dtype=np.float64)
    rel = i[:, None] - i[None, :]
    decay = np.where(rel >= 0, np.exp(log_gamma[:, None, None] * np.maximum(rel, 0.0)), 0.0)
    k_to_end = np.exp((c - 1.0 - i)[None, :] * log_gamma[:, None])
    q_from_start = np.exp((i + 1.0)[None, :] * log_gamma[:, None])
    kend = np.broadcast_to(k_to_end[:, :, None], (H, c, dk))
    qst = np.broadcast_to(q_from_start[:, :, None], (H, c, dk))
    cdec = np.broadcast_to(np.exp(c * log_gamma)[:, None, None], (H, 1, RET_DV))
    return tuple(jnp.asarray(t, F32) for t in (cos2, sin2, decay, kend, qst, cdec))


def _retention(pm, tables, gain):
    B, S, _ = pm.shape
    H, dk, dv, c = RET_HEADS, RET_DK, RET_DV, RET_CHUNK
    cos2, sin2, decay, kend, qst, cdec = tables
    qb = _M_RET // dk
    kb = qb + H
    vb = (_M_RET + 2 * H * dk) // dv
    gb = vb + H
    const2 = lambda b, h: (0, 0)
    head3 = lambda b, h: (h, 0, 0)
    return pl.pallas_call(
        _ret_kernel,
        grid=(B, H),
        in_specs=[pl.BlockSpec((1, S, dk), lambda b, h: (b, 0, qb + h)),
                  pl.BlockSpec((1, S, dk), lambda b, h: (b, 0, kb + h)),
                  pl.BlockSpec((1, S, dv), lambda b, h: (b, 0, vb + h)),
                  pl.BlockSpec((1, S, dv), lambda b, h: (b, 0, gb + h)),
                  pl.BlockSpec((S, dk), const2),
                  pl.BlockSpec((S, dk), const2),
                  pl.BlockSpec((1, c, c), head3),
                  pl.BlockSpec((1, c, dk), head3),
                  pl.BlockSpec((1, c, dk), head3),
                  pl.BlockSpec((1, 1, dv), head3),
                  pl.BlockSpec((1, dv), lambda b, h: (0, h))],
        out_specs=pl.BlockSpec((1, S, dv), lambda b, h: (b, 0, h)),
        out_shape=jax.ShapeDtypeStruct((B, S, H * dv), BF16),
        compiler_params=_cparams(("parallel", "parallel")),
        name="retention",
    )(pm, pm, pm, pm, cos2, sin2, decay, kend, qst, cdec, gain)


def _diff_kernel(q_ref, k_ref, v_ref, bias_ref, far_ref, lam_ref, gain_ref, o_ref, vv_ref, *,
                 lam_init):
    S = q_ref.shape[1]
    QB = DIL_BLOCK
    dv = 2 * DIFF_DH
    vv_ref[:, :dv] = v_ref[0]
    vv_ref[:, dv:] = jnp.ones((S, dv), BF16)
    first = lax.broadcasted_iota(jnp.int32, (QB, 2 * DIFF_DH), 1) < DIFF_DH
    lp = lam_ref[...]
    lam = (jnp.exp(jnp.sum(lp[0:1] * lp[1:2], axis=-1, keepdims=True))
           - jnp.exp(jnp.sum(lp[2:3] * lp[3:4], axis=-1, keepdims=True)) + lam_init)
    bias = (bias_ref[0] - far_ref[0][0:1, 0:1]) * _LOG2E
    bias2 = jnp.concatenate([bias, bias], axis=0)
    gain = gain_ref[...] * (1.0 - lam_init)
    zero = jnp.zeros((), BF16)
    n_q = S // QB

    def bounds(qi):
        return max(qi - 1, 0) * QB, (qi + 1) * QB

    def scores(qi):
        qb = (q_ref[0, qi * QB:(qi + 1) * QB, :].astype(F32)
              * (DIFF_DH ** -0.5 * _LOG2E)).astype(BF16)
        qq = jnp.concatenate([jnp.where(first, qb, zero), jnp.where(first, zero, qb)], axis=0)
        lo, hi = bounds(qi)
        sb = _dot_nt(qq, k_ref[0, lo:hi, :]) + (bias2 if qi > 0 else bias2[:, QB:])
        sf = _dot_nt(qq, k_ref[0, 0:lo, :]) if lo > 0 else None
        return sb, sf

    def finish(qi, sb, sf):
        lo, hi = bounds(qi)
        m = jnp.max(sb, axis=-1, keepdims=True)
        if sf is not None:
            m = jnp.maximum(m, jnp.max(sf, axis=-1, keepdims=True))
        acc = _dot(jnp.exp2((sb - m).astype(BF16)), vv_ref[lo:hi, :])
        if sf is not None:
            acc = acc + _dot(jnp.exp2((sf - m).astype(BF16)), vv_ref[0:lo, :])
        r = acc[:, :dv] / acc[:, dv:]
        o = r[:QB] - lam * r[QB:]
        o = o * lax.rsqrt(jnp.mean(o * o, axis=-1, keepdims=True) + DIFF_EPS) * gain
        o_ref[0, qi * QB:(qi + 1) * QB, :] = o.astype(o_ref.dtype)

    pending = {}
    for step in range(n_q + _DIFF_AHEAD):
        if step < n_q:
            pending[step] = scores(step)
        if step >= _DIFF_AHEAD:
            finish(step - _DIFF_AHEAD, *pending.pop(step - _DIFF_AHEAD))


def _diff_attention(pm, tiles, far, lam_params, gain, lam_init):
    B, S, _ = pm.shape
    H, w = DIFF_HEADS, 2 * DIFF_DH
    qb = _M_DIFF // w
    kb, vb = qb + H, qb + 2 * H
    return pl.pallas_call(
        functools.partial(_diff_kernel, lam_init=lam_init),
        grid=(B, H),
        in_specs=[pl.BlockSpec((1, S, w), lambda b, h: (b, 0, qb + h)),
                  pl.BlockSpec((1, S, w), lambda b, h: (b, 0, kb + h)),
                  pl.BlockSpec((1, S, w), lambda b, h: (b, 0, vb + h)),
                  pl.BlockSpec((1, DIL_BLOCK, 2 * DIL_BLOCK), lambda b, h: (h, 0, 0)),
                  pl.BlockSpec((1, 8, LANE), lambda b, h: (h, 0, 0)),
                  pl.BlockSpec((4, DIFF_DH), lambda b, h: (0, 0)),
                  pl.BlockSpec((1, w), lambda b, h: (0, 0))],
        out_specs=pl.BlockSpec((1, S, w), lambda b, h: (b, 0, h)),
        out_shape=jax.ShapeDtypeStruct((B, S, H * w), BF16),
        scratch_shapes=[pltpu.VMEM((S, 2 * w), BF16)],
        compiler_params=_cparams(("parallel", "parallel")),
        name="diff_attention",
    )(pm, pm, pm, tiles, far, lam_params, gain)


def _run_pipelined(stages):
    pending = None
    for issue, consume in list(stages) + [(None, None)]:
        issued = issue() if issue is not None else None
        if pending is not None:
            pending[1](pending[0])
        pending = (issued, consume) if consume is not None else None


def _softmax_parts(s, bias):
    s = s + bias
    mx = jnp.max(s, axis=-1, keepdims=True)
    p = jnp.exp2((s - mx) * (_DIL_SCALE * _LOG2E))
    return p.astype(BF16), mx * _DIL_SCALE, jnp.sum(p, axis=-1, keepdims=True)


def _lse_combine(o_a, l_a, o_b, l_b):
    mx = jnp.maximum(l_a, l_b)
    w_a, w_b = jnp.exp(l_a - mx), jnp.exp(l_b - mx)
    den = w_a + w_b
    return (w_a * o_a + w_b * o_b) / den, mx + jnp.log(den)


def _dil_kernel(q0_ref, k0_ref, v0_ref, q1_ref, k1_ref, v1_ref, q2_ref, k2_ref, v2_ref,
                t0_ref, t1_ref, t1f_ref, t2_ref, unperm_ref, y_ref,
                o1_ref, l1_ref, x16_ref, nat_ref):
    L, dh = DIL_BLOCK, DIL_DH
    S = q0_ref.shape[1]
    n_a = S // RES
    dil1 = DIL_PATTERNS[1][1]
    n_str = RES // dil1
    qa = L // n_str
    lane = lax.broadcasted_iota(jnp.int32, (L, dh), 1)
    zeros = jnp.zeros((L, dh), BF16)
    stages = []

    def g1_stage(r4):
        streams = [r4 + dil1 * s for s in range(n_str)]
        n_blocks = n_a // qa

        def gather(ref, lo, hi):
            return jnp.concatenate([ref[0, r, lo:hi, :] for r in streams], axis=0)

        def band(n):
            return (0, qa) if n == 0 else ((n - 1) * qa, (n + 1) * qa)

        def issue():
            return [_dot_nt(gather(q1_ref, n * qa, (n + 1) * qa), gather(k1_ref, *band(n)))
                    for n in range(n_blocks)]

        def consume(ss):
            parts = [_softmax_parts(s, t1f_ref[0][:, :L] if n == 0 else t1_ref[0])
                     for n, s in enumerate(ss)]
            nums = [_dot(p, gather(v1_ref, *band(n))) for n, (p, _, _) in enumerate(parts)]
            for n, (num, (_, mx, den)) in enumerate(zip(nums, parts)):
                o = num / den
                lse = jnp.broadcast_to(mx + jnp.log(den), (L, dh))
                for i, r in enumerate(streams):
                    o1_ref[r, n * qa:(n + 1) * qa, :] = o[i * qa:(i + 1) * qa]
                    l1_ref[r, n * qa:(n + 1) * qa, :] = lse[i * qa:(i + 1) * qa]

        return issue, consume

    stages += [g1_stage(r4) for r4 in range(dil1)]

    def g2_stage(pairs):
        def blockdiag(ref, r):
            return jnp.concatenate([jnp.concatenate([ref[0, r], zeros], axis=1),
                                    jnp.concatenate([zeros, ref[0, r + 1]], axis=1)], axis=0)

        def issue():
            return [_dot_nt(jnp.concatenate([q2_ref[0, r], q2_ref[0, r + 1]], axis=1),
                            blockdiag(k2_ref, r)) for r in pairs]

        def consume(ss):
            bias = t2_ref[0][:, L:]
            parts = [[_softmax_parts(s[:, i * L:(i + 1) * L], bias) for i in range(2)] for s in ss]
            outs = [_dot(jnp.concatenate([pp[0][0], pp[1][0]], axis=1), blockdiag(v2_ref, r))
                    for r, pp in zip(pairs, parts)]
            for r, pp, out in zip(pairs, parts, outs):
                for i in range(2):
                    _, mx, den = pp[i]
                    o2 = out[:, i * L:(i + 1) * L] / den
                    l2 = jnp.broadcast_to(mx + jnp.log(den), (L, dh))
                    o12, l12 = _lse_combine(o1_ref[r + i], l1_ref[r + i], o2, l2)
                    hi = l12.astype(BF16)
                    lo = (l12 - hi.astype(F32)).astype(BF16)
                    x16_ref[r + i] = jnp.concatenate(
                        [o12.astype(BF16), jnp.where(lane < dh // 2, hi, lo)], axis=1)

        return issue, consume

    stages += [g2_stage(range(r0, r0 + RES // 2, 2)) for r0 in (0, RES // 2)]

    def unpermute(_):
        for s in range(S // PERM_ROWS):
            xs = jnp.concatenate([x16_ref[r, s * RES:(s + 1) * RES, :] for r in range(RES)], axis=0)
            nat_ref[s * PERM_ROWS:(s + 1) * PERM_ROWS, :] = _dot(unperm_ref[...], xs)

    stages.append((lambda: None, unpermute))

    def g0_stage(blocks):
        def band(n):
            return (0, L) if n == 0 else ((n - 1) * L, (n + 1) * L)

        def issue():
            return [_dot_nt(q0_ref[0, n * L:(n + 1) * L, :], k0_ref[0, band(n)[0]:band(n)[1], :])
                    for n in blocks]

        def consume(ss):
            parts = [_softmax_parts(s, t0_ref[0][:, L:] if n == 0 else t0_ref[0])
                     for n, s in zip(blocks, ss)]
            nums = [_dot(p, v0_ref[0, band(n)[0]:band(n)[1], :])
                    for n, (p, _, _) in zip(blocks, parts)]
            for n, num, (_, mx, den) in zip(blocks, nums, parts):
                nat = nat_ref[n * L:(n + 1) * L, :]
                l12 = nat[:, dh:dh + 1] + nat[:, dh + dh // 2:dh + dh // 2 + 1]
                y, _ = _lse_combine(num / den, mx + jnp.log(den), nat[:, :dh], l12)
                y_ref[0, n * L:(n + 1) * L, :] = y.astype(y_ref.dtype)

        return issue, consume

    per_stage = 4
    stages += [g0_stage(range(n0, n0 + per_stage)) for n0 in range(0, S // L, per_stage)]
    _run_pipelined(stages)


def _dilated(pm, p16, tiles, unperm):
    B, S, _ = pm.shape
    L, dh, H = DIL_BLOCK, DIL_DH, DIL_HEADS
    n_a = S // RES
    c0 = _M_DIL0 // dh
    nat = lambda c: pl.BlockSpec((1, S, dh), lambda b, h: (b, 0, c + h))
    res = lambda c: pl.BlockSpec((1, RES, n_a, dh), lambda b, h: (b, 0, 0, c + h))
    tile = lambda t: pl.BlockSpec((1, L, 2 * L), lambda b, h: (t + h, 0, 0))
    return pl.pallas_call(
        _dil_kernel,
        grid=(B, H),
        in_specs=[nat(c0), nat(c0 + H), nat(c0 + 2 * H),
                  res(0), res(H), res(2 * H), res(3 * H), res(4 * H), res(5 * H),
                  tile(_TILE_G0), tile(_TILE_G1), tile(_TILE_G1_FIRST), tile(_TILE_G2),
                  pl.BlockSpec((PERM_ROWS, PERM_ROWS), lambda b, h: (0, 0))],
        out_specs=pl.BlockSpec((1, S, dh), lambda b, h: (b, 0, h)),
        out_shape=jax.ShapeDtypeStruct((B, S, H * dh), BF16),
        scratch_shapes=[pltpu.VMEM((RES, n_a, dh), F32), pltpu.VMEM((RES, n_a, dh), F32),
                        pltpu.VMEM((RES, n_a, 2 * dh), BF16), pltpu.VMEM((S, 2 * dh), F32)],
        compiler_params=_cparams(("parallel", "parallel")),
        name="dilated",
    )(pm, pm, pm, p16, p16, p16, p16, p16, p16, tiles, tiles, tiles, tiles, unperm)


def _merge_kernel(gate_ref, yr_ref, yd_ref, yl_ref, x_ref, wr_ref, wd_ref, wl_ref, wo_ref, out_ref):
    D = D_MODEL
    gates = jax.nn.sigmoid(gate_ref[...].astype(F32))
    merged = (gates[:, 0:D] * _dot(yr_ref[...], wr_ref[...])
              + gates[:, D:2 * D] * _dot(yd_ref[...], wd_ref[...])
              + gates[:, 2 * D:3 * D] * _dot(yl_ref[...], wl_ref[...]))
    out_ref[...] = x_ref[...] + _dot(merged.astype(BF16), wo_ref[...])


def _layer_spec(a, l):
    return pl.BlockSpec((None,) + a.shape[1:], lambda i: (l, 0, 0), pipeline_mode=pl.Buffered(1))


def _merge(pm2, y_ret, y_diff, y_dil, x2, wr, wd, wl, wo, l, tm=512):
    T, D = x2.shape
    row = lambda w: pl.BlockSpec((tm, w), lambda i: (i, 0))
    return pl.pallas_call(
        _merge_kernel,
        grid=(T // tm,),
        in_specs=[row(_GATE_W), row(D), row(D), row(y_dil.shape[1]), row(D),
                  _layer_spec(wr, l), _layer_spec(wd, l), _layer_spec(wl, l), _layer_spec(wo, l)],
        out_specs=row(D),
        out_shape=jax.ShapeDtypeStruct((T, D), F32),
        compiler_params=_cparams(("parallel",)),
        name="merge_out",
    )(pm2, y_ret, y_diff, y_dil, x2, wr, wd, wl, wo)


def _ffn_kernel(x_ref, g_ref, wg_ref, wu_ref, wd_ref, gf_ref, o_ref, *, final):
    x = x_ref[...]
    h = _rms(x, g_ref[...], RMS_EPS).astype(BF16)
    a = _dot(h, wg_ref[...])
    u = _dot(h, wu_ref[...])
    z = (a * jax.nn.sigmoid(a) * u).astype(BF16)
    y = x + _dot(z, wd_ref[...])
    if final:
        y = _rms(y, gf_ref[...], RMS_EPS)
    o_ref[...] = y


def _ffn(x2, g, wg, wu, wd, g_final, l, final, tm=512):
    T, D = x2.shape
    row = pl.BlockSpec((tm, D), lambda i: (i, 0))
    return pl.pallas_call(
        functools.partial(_ffn_kernel, final=final),
        grid=(T // tm,),
        in_specs=[row, _layer_spec(g, l), _layer_spec(wg, l), _layer_spec(wu, l),
                  _layer_spec(wd, l), _layer_spec(g_final, 0)],
        out_specs=row,
        out_shape=jax.ShapeDtypeStruct((T, D), F32),
        compiler_params=_cparams(("parallel",)),
        name="ffn_final" if final else "ffn",
    )(x2, g, wg, wu, wd, g_final)


def kernel(x, w_in, w_branch_ret, w_branch_diff, w_branch_dil, w_out, norm_mix, norm_ffn, ret_gn_gain, diff_lambda, diff_subln_gain, rel_bias, w_ffn_gate, w_ffn_up, w_ffn_down, norm_final):
    B, S, D = x.shape
    T = B * S
    tiles, far = _bias_tiles(rel_bias)
    ret_tables = _ret_tables(S)
    perm = _residue_perm()
    perm_fwd, perm_back = jnp.asarray(perm, BF16), jnp.asarray(perm.T, BF16)
    x2 = x.reshape(T, D)
    w_in, w_branch_ret, w_branch_diff, w_branch_dil, w_out, w_ffn_gate, w_ffn_up, w_ffn_down = (
        w.astype(BF16) for w in (w_in, w_branch_ret, w_branch_diff, w_branch_dil, w_out,
                                 w_ffn_gate, w_ffn_up, w_ffn_down))
    norm_mix, norm_ffn = norm_mix.reshape(DEPTH, 1, D), norm_ffn.reshape(DEPTH, 1, D)
    norm_final = norm_final.reshape(1, 1, D)
    for l in range(DEPTH):
        pm2, p16 = _inproj(x2, norm_mix, perm_fwd, w_in, l, B)
        pm = pm2.reshape(B, S, _MAIN_W)

        y_ret = _retention(pm, ret_tables, ret_gn_gain[l].reshape(1, -1))
        lam_init = 0.8 - 0.6 * math.exp(-0.3 * l)
        y_diff = _diff_attention(pm, tiles, far, diff_lambda[l], diff_subln_gain[l].reshape(1, -1),
                                 lam_init)
        y_dil = _dilated(pm, p16, tiles, perm_back)

        x2 = _merge(pm2, y_ret.reshape(T, -1), y_diff.reshape(T, -1), y_dil.reshape(T, -1), x2,
                    w_branch_ret, w_branch_diff, w_branch_dil, w_out, l)
        x2 = _ffn(x2, norm_ffn, w_ffn_gate, w_ffn_up, w_ffn_down, norm_final, l,
                  final=(l == DEPTH - 1))
    return x2.reshape(B, S, D)
```

```python
import functools
import math

import numpy as np
import jax
import jax.numpy as jnp
from jax import lax
from jax.experimental import pallas as pl
from jax.experimental.pallas import tpu as pltpu

F32 = jnp.float32
BF16 = jnp.bfloat16

D_MODEL = 1024
DEPTH = 2
RET_HEADS, RET_DK, RET_DV, RET_CHUNK = 4, 128, 256, 128
ROPE_BASE = 10000.0
GN_EPS = 1e-5
DIFF_HEADS, DIFF_DH = 8, 64
DIFF_EPS = 1e-5
DIL_PATTERNS = ((128, 1), (512, 4), (2048, 16))
N_DIL = len(DIL_PATTERNS)
DIL_HEADS, DIL_DH, DIL_BLOCK = 4, 128, 128
REL_BUCKETS, REL_MAX_DIST = 32, 128
N_BIAS_HEADS = DIFF_HEADS + N_DIL * DIL_HEADS
FFN_HIDDEN = -(-8 * D_MODEL // (3 * 256)) * 256
RMS_EPS = 1e-6
NEG_INF = -1e30

_RET_W = 2 * RET_HEADS * RET_DK + 2 * RET_HEADS * RET_DV
_DIFF_W = 3 * DIFF_HEADS * 2 * DIFF_DH
_DIL_W = 3 * DIL_HEADS * DIL_DH
_GATE_W = 3 * D_MODEL
_OFF_DIFF = _RET_W
_OFF_DIL = _RET_W + _DIFF_W
_OFF_GATE = _OFF_DIL + N_DIL * _DIL_W
_MAIN_W = _GATE_W + _RET_W + _DIFF_W + _DIL_W
_M_RET = _GATE_W
_M_DIFF = _GATE_W + _RET_W
_M_DIL0 = _GATE_W + _RET_W + _DIFF_W

_LOG2E = math.log2(math.e)
_DIL_SCALE = DIL_DH ** -0.5
_DIFF_AHEAD = 3
_DIFF_HEADS_PER_STEP = 2

LANE = 128
VMEM_LIMIT = 56 * 1024 * 1024


def _cparams(sem):
    return pltpu.CompilerParams(dimension_semantics=sem, vmem_limit_bytes=VMEM_LIMIT)


def _rms(x, g, eps):
    return x * lax.rsqrt(jnp.mean(x * x, axis=-1, keepdims=True) + eps) * g


def _dot(a, b):
    return jnp.dot(a, b, preferred_element_type=F32)


def _dot_nt(a, b):
    return lax.dot_general(a, b, (((1,), (1,)), ((), ())), preferred_element_type=F32)


def _dot_tn(a, b):
    return lax.dot_general(a, b, (((0,), (0,)), ((), ())), preferred_element_type=F32)


RES = 16
PERM_ROWS = RES * RES


def _residue_perm():
    p = np.zeros((PERM_ROWS, PERM_ROWS), np.float32)
    a, r = np.meshgrid(np.arange(RES), np.arange(RES), indexing="ij")
    p[(r * RES + a).ravel(), (a * RES + r).ravel()] = 1.0
    return p


def _inproj_kernel(x_ref, g_ref, perm_ref, w_ref, o_ref, h_ref, hp_ref, *, n_main, row_chunk):
    j = pl.program_id(1)
    S = x_ref.shape[0]
    per_res = S // RES

    @pl.when(j == 0)
    def _():
        for s in range(S // PERM_ROWS):
            rows = pl.ds(s * PERM_ROWS, PERM_ROWS)
            h = _rms(x_ref[rows, :], g_ref[...], RMS_EPS).astype(BF16)
            h_ref[rows, :] = h
            hs = _dot(perm_ref[...], h).astype(BF16)
            for r in range(RES):
                hp_ref[pl.ds(r * per_res + s * RES, RES), :] = hs[r * RES:(r + 1) * RES]

    def project(src_ref):
        for c in range(S // row_chunk):
            rows = pl.ds(c * row_chunk, row_chunk)
            o_ref[rows, :] = _dot(src_ref[rows, :], w_ref[...]).astype(o_ref.dtype)

    @pl.when(j < n_main)
    def _():
        project(h_ref)

    @pl.when(j >= n_main)
    def _():
        project(hp_ref)


def _inproj(x2, g, perm, w, l, B, tn=1536, row_chunk=1024):
    T, D = x2.shape
    S = T // B
    n_main = _MAIN_W // tn
    n_tiles = w.shape[2] // tn
    first = _OFF_GATE // tn
    return pl.pallas_call(
        functools.partial(_inproj_kernel, n_main=n_main, row_chunk=row_chunk),
        grid=(B, n_tiles),
        in_specs=[pl.BlockSpec((S, D), lambda i, j: (i, 0)),
                  pl.BlockSpec((None, 1, D), lambda i, j: (l, 0, 0)),
                  pl.BlockSpec((PERM_ROWS, PERM_ROWS), lambda i, j: (0, 0)),
                  pl.BlockSpec((None, D, tn), lambda i, j: (l, 0, (j + first) % n_tiles))],
        out_specs=pl.BlockSpec((S, tn), lambda i, j: (i, j)),
        out_shape=jax.ShapeDtypeStruct((T, w.shape[2]), BF16),
        scratch_shapes=[pltpu.VMEM((S, D), BF16), pltpu.VMEM((S, D), BF16)],
        compiler_params=_cparams(("parallel", "arbitrary")),
        name="inproj",
    )(x2, g, perm, w)


def _bucket_patterns():
    L = DIL_BLOCK
    i = np.arange(L)[:, None]
    j = np.arange(2 * L)[None, :]
    m = i + L - j
    n_str = RES // DIL_PATTERNS[1][1]
    qa = L // n_str
    u_q = 4 * (i % qa) + i // qa
    m_g1 = u_q + L - (4 * (j % (2 * qa)) + j // (2 * qa))
    m_g1_first = np.where(j < L, u_q - (4 * (j % qa) + j // qa), -1)

    def bucket(dist):
        n = np.maximum(dist, 0)
        exact = REL_BUCKETS // 2
        log_ratio = (np.log(np.maximum(n, exact).astype(np.float32) / np.float32(exact))
                     / np.float32(math.log(REL_MAX_DIST / exact))).astype(np.float32)
        large = np.minimum(exact + (log_ratio * np.float32(REL_BUCKETS - exact)).astype(np.int32),
                           REL_BUCKETS - 1)
        return np.where(n < exact, n, large).astype(np.int32)

    def windowed(mm, gi):
        window, dil = DIL_PATTERNS[gi]
        return np.where((mm >= 0) & (mm <= window // dil), bucket(mm * dil), -1)

    pats = [np.where(m >= 0, bucket(m), -1), windowed(m, 0), windowed(m_g1, 1),
            windowed(m_g1_first, 1), windowed(m, 2)]
    return np.stack(pats).astype(np.int32)


_TILE_HEADS = (list(range(DIFF_HEADS + 2 * DIL_HEADS))
               + list(range(DIFF_HEADS + DIL_HEADS, DIFF_HEADS + 3 * DIL_HEADS)))
_TILE_PATS = [0] * DIFF_HEADS + [1] * DIL_HEADS + [2] * DIL_HEADS + [3] * DIL_HEADS + [4] * DIL_HEADS
_TILE_G0, _TILE_G1, _TILE_G1_FIRST, _TILE_G2 = (DIFF_HEADS + k * DIL_HEADS for k in range(4))


def _bias_kernel(head_ref, patid_ref, tbl_ref, pat_ref, tile_ref, far_ref):
    t = pl.program_id(0)
    h = head_ref[t]
    pat = pat_ref[0]
    mult = jnp.where(patid_ref[t] == 0, 1.0, 1.0 / _DIL_SCALE).astype(F32)
    tile = jnp.full(pat.shape, NEG_INF, F32)
    for b in range(REL_BUCKETS):
        tile = jnp.where(pat == b, tbl_ref[h, b] * mult, tile)
    tile_ref[0] = tile
    far_ref[0] = jnp.full(far_ref.shape[1:], tbl_ref[h, REL_BUCKETS - 1], F32)


def _bias_tiles(rel_bias):
    pats = jnp.asarray(_bucket_patterns())
    tbl = rel_bias.T
    n_tiles = len(_TILE_HEADS)
    shape = (1, DIL_BLOCK, 2 * DIL_BLOCK)
    return pl.pallas_call(
        _bias_kernel,
        grid_spec=pltpu.PrefetchScalarGridSpec(
            num_scalar_prefetch=2,
            grid=(n_tiles,),
            in_specs=[pl.BlockSpec(memory_space=pltpu.SMEM),
                      pl.BlockSpec(shape, lambda t, heads, patids: (patids[t], 0, 0))],
            out_specs=[pl.BlockSpec(shape, lambda t, heads, patids: (t, 0, 0)),
                       pl.BlockSpec((1, 8, LANE), lambda t, heads, patids: (t, 0, 0))]),
        out_shape=[jax.ShapeDtypeStruct((n_tiles,) + shape[1:], F32),
                   jax.ShapeDtypeStruct((n_tiles, 8, LANE), F32)],
        compiler_params=_cparams(("arbitrary",)),
        name="bias_tiles",
    )(jnp.asarray(_TILE_HEADS, jnp.int32), jnp.asarray(_TILE_PATS, jnp.int32), tbl, pats)


def _ret_kernel(q_ref, k_ref, v_ref, g_ref, cos_ref, sin_ref, dec_ref, kend_ref, qst_ref, cdec_ref,
                gain_ref, o_ref):
    c = RET_CHUNK
    n_chunks = q_ref.shape[1] // c
    scores, q_cross, kvs = [], [], []
    for n in range(n_chunks):
        rows = pl.ds(n * c, c)
        cos = cos_ref[rows, :]
        sin = sin_ref[rows, :]
        q = q_ref[0, rows, :].astype(F32)
        k = k_ref[0, rows, :].astype(F32)
        qr = q * cos + pltpu.roll(q, RET_DK // 2, 1) * sin
        kr = (k * cos + pltpu.roll(k, RET_DK // 2, 1) * sin) * (RET_DK ** -0.5)
        scores.append(_dot_nt(qr.astype(BF16), kr.astype(BF16)))
        q_cross.append((qr * qst_ref[0]).astype(BF16))
        if n < n_chunks - 1:
            kvs.append(_dot_tn((kr * kend_ref[0]).astype(BF16), v_ref[0, rows, :]))

    states = [None]
    st = None
    for n in range(n_chunks - 1):
        st = kvs[n] if st is None else st * cdec_ref[0] + kvs[n]
        states.append(st.astype(BF16))

    for n in range(n_chunks):
        rows = pl.ds(n * c, c)
        y = _dot((scores[n] * dec_ref[0]).astype(BF16), v_ref[0, rows, :])
        if states[n] is not None:
            y = y + _dot(q_cross[n], states[n])
        mu = jnp.mean(y, axis=-1, keepdims=True)
        yc = y - mu
        var = jnp.mean(yc * yc, axis=-1, keepdims=True)
        yn = yc * lax.rsqrt(var + GN_EPS) * gain_ref[...]
        g = g_ref[0, rows, :]
        o_ref[0, rows, :] = ((g * jax.nn.sigmoid(g)).astype(F32) * yn).astype(o_ref.dtype)


def _ret_tables(S):
    H, dk, c = RET_HEADS, RET_DK, RET_CHUNK
    half = dk // 2
    pos = np.arange(S, dtype=np.float64)
    inv = ROPE_BASE ** (-np.arange(half, dtype=np.float64) / half)
    ang = pos[:, None] * inv[None, :]
    cos, sin = np.cos(ang), np.sin(ang)
    cos2 = np.concatenate([cos, cos], axis=-1)
    sin2 = np.concatenate([-sin, sin], axis=-1)
    log_gamma = np.log1p(-np.exp2(-5.0 - np.arange(H, dtype=np.float64)))
    i = np.arange(c, dtype=np.float64)
    rel = i[:, None] - i[None, :]
    decay = np.where(rel >= 0, np.exp(log_gamma[:, None, None] * np.maximum(rel, 0.0)), 0.0)
    k_to_end = np.exp((c - 1.0 - i)[None, :] * log_gamma[:, None])
    q_from_start = np.exp((i + 1.0)[None, :] * log_gamma[:, None])
    kend = np.broadcast_to(k_to_end[:, :, None], (H, c, dk))
    qst = np.broadcast_to(q_from_start[:, :, None], (H, c, dk))
    cdec = np.broadcast_to(np.exp(c * log_gamma)[:, None, None], (H, 1, RET_DV))
    return tuple(jnp.asarray(t, F32) for t in (cos2, sin2, decay, kend, qst, cdec))


def _retention(pm, tables, gain):
    B, S, _ = pm.shape
    H, dk, dv, c = RET_HEADS, RET_DK, RET_DV, RET_CHUNK
    cos2, sin2, decay, kend, qst, cdec = tables
    qb = _M_RET // dk
    kb = qb + H
    vb = (_M_RET + 2 * H * dk) // dv
    gb = vb + H
    const2 = lambda b, h: (0, 0)
    head3 = lambda b, h: (h, 0, 0)
    return pl.pallas_call(
        _ret_kernel,
        grid=(B, H),
        in_specs=[pl.BlockSpec((1, S, dk), lambda b, h: (b, 0, qb + h)),
                  pl.BlockSpec((1, S, dk), lambda b, h: (b, 0, kb + h)),
                  pl.BlockSpec((1, S, dv), lambda b, h: (b, 0, vb + h)),
                  pl.BlockSpec((1, S, dv), lambda b, h: (b, 0, gb + h)),
                  pl.BlockSpec((S, dk), const2),
                  pl.BlockSpec((S, dk), const2),
                  pl.BlockSpec((1, c, c), head3),
                  pl.BlockSpec((1, c, dk), head3),
                  pl.BlockSpec((1, c, dk), head3),
                  pl.BlockSpec((1, 1, dv), head3),
                  pl.BlockSpec((1, dv), lambda b, h: (0, h))],
        out_specs=pl.BlockSpec((1, S, dv), lambda b, h: (b, 0, h)),
        out_shape=jax.ShapeDtypeStruct((B, S, H * dv), BF16),
        compiler_params=_cparams(("parallel", "parallel")),
        name="retention",
    )(pm, pm, pm, pm, cos2, sin2, decay, kend, qst, cdec, gain)


def _diff_kernel(q_ref, k_ref, v_ref, bias_ref, far_ref, lam_ref, gain_ref, o_ref, vv_ref, *,
                 lam_init):
    for hh in range(_DIFF_HEADS_PER_STEP):
        _diff_head(q_ref, k_ref, v_ref, bias_ref, far_ref, lam_ref, gain_ref, o_ref, vv_ref, hh,
                   lam_init)


def _diff_head(q_ref, k_ref, v_ref, bias_ref, far_ref, lam_ref, gain_ref, o_ref, vv_ref, hh,
               lam_init):
    S = q_ref.shape[1]
    QB = DIL_BLOCK
    dv = 2 * DIFF_DH
    cs = slice(hh * dv, (hh + 1) * dv)
    vv_ref[hh, :, :dv] = v_ref[0, :, cs]
    vv_ref[hh, :, dv:] = jnp.ones((S, dv), BF16)
    first = lax.broadcasted_iota(jnp.int32, (QB, 2 * DIFF_DH), 1) < DIFF_DH
    lp = lam_ref[...]
    lam = (jnp.exp(jnp.sum(lp[0:1] * lp[1:2], axis=-1, keepdims=True))
           - jnp.exp(jnp.sum(lp[2:3] * lp[3:4], axis=-1, keepdims=True)) + lam_init)
    bias = (bias_ref[hh] - far_ref[hh][0:1, 0:1]) * _LOG2E
    bias2 = jnp.concatenate([bias, bias], axis=0)
    gain = gain_ref[...] * (1.0 - lam_init)
    zero = jnp.zeros((), BF16)
    n_q = S // QB

    def bounds(qi):
        return max(qi - 1, 0) * QB, (qi + 1) * QB

    def scores(qi):
        qb = (q_ref[0, qi * QB:(qi + 1) * QB, cs].astype(F32)
              * (DIFF_DH ** -0.5 * _LOG2E)).astype(BF16)
        qq = jnp.concatenate([jnp.where(first, qb, zero), jnp.where(first, zero, qb)], axis=0)
        lo, hi = bounds(qi)
        sb = _dot_nt(qq, k_ref[0, lo:hi, cs]) + (bias2 if qi > 0 else bias2[:, QB:])
        sf = _dot_nt(qq, k_ref[0, 0:lo, cs]) if lo > 0 else None
        return sb, sf

    def finish(qi, sb, sf):
        lo, hi = bounds(qi)
        m = jnp.max(sb, axis=-1, keepdims=True)
        if sf is not None:
            m = jnp.maximum(m, jnp.max(sf, axis=-1, keepdims=True))
        acc = _dot(jnp.exp2((sb - m).astype(BF16)), vv_ref[hh, lo:hi, :])
        if sf is not None:
            acc = acc + _dot(jnp.exp2((sf - m).astype(BF16)), vv_ref[hh, 0:lo, :])
        r = acc[:, :dv] / acc[:, dv:]
        o = r[:QB] - lam * r[QB:]
        o = o * lax.rsqrt(jnp.mean(o * o, axis=-1, keepdims=True) + DIFF_EPS) * gain
        o_ref[0, qi * QB:(qi + 1) * QB, cs] = o.astype(o_ref.dtype)

    pending = {}
    for step in range(n_q + _DIFF_AHEAD):
        if step < n_q:
            pending[step] = scores(step)
        if step >= _DIFF_AHEAD:
            finish(step - _DIFF_AHEAD, *pending.pop(step - _DIFF_AHEAD))


def _diff_attention(pm, tiles, far, lam_params, gain, lam_init):
    B, S, _ = pm.shape
    H, hw = DIFF_HEADS, 2 * DIFF_DH
    n = _DIFF_HEADS_PER_STEP
    w = n * hw
    qb = _M_DIFF // w
    kb, vb = qb + H // n, qb + 2 * (H // n)
    return pl.pallas_call(
        functools.partial(_diff_kernel, lam_init=lam_init),
        grid=(B, H // n),
        in_specs=[pl.BlockSpec((1, S, w), lambda b, h: (b, 0, qb + h)),
                  pl.BlockSpec((1, S, w), lambda b, h: (b, 0, kb + h)),
                  pl.BlockSpec((1, S, w), lambda b, h: (b, 0, vb + h)),
                  pl.BlockSpec((n, DIL_BLOCK, 2 * DIL_BLOCK), lambda b, h: (h, 0, 0)),
                  pl.BlockSpec((n, 8, LANE), lambda b, h: (h, 0, 0)),
                  pl.BlockSpec((4, DIFF_DH), lambda b, h: (0, 0)),
                  pl.BlockSpec((1, hw), lambda b, h: (0, 0))],
        out_specs=pl.BlockSpec((1, S, w), lambda b, h: (b, 0, h)),
        out_shape=jax.ShapeDtypeStruct((B, S, H * hw), BF16),
        scratch_shapes=[pltpu.VMEM((n, S, 2 * hw), BF16)],
        compiler_params=_cparams(("parallel", "parallel")),
        name="diff_attention",
    )(pm, pm, pm, tiles, far, lam_params, gain)


def _run_pipelined(stages):
    pending = None
    for issue, consume in list(stages) + [(None, None)]:
        issued = issue() if issue is not None else None
        if pending is not None:
            pending[1](pending[0])
        pending = (issued, consume) if consume is not None else None


def _softmax_parts(s, bias):
    s = s + bias
    mx = jnp.max(s, axis=-1, keepdims=True)
    p = jnp.exp2((s - mx) * (_DIL_SCALE * _LOG2E))
    return p.astype(BF16), mx * _DIL_SCALE, jnp.sum(p, axis=-1, keepdims=True)


def _lse_combine(o_a, l_a, o_b, l_b):
    mx = jnp.maximum(l_a, l_b)
    w_a, w_b = jnp.exp(l_a - mx), jnp.exp(l_b - mx)
    den = w_a + w_b
    return (w_a * o_a + w_b * o_b) / den, mx + jnp.log(den)


def _dil_kernel(q0_ref, k0_ref, v0_ref, q1_ref, k1_ref, v1_ref, q2_ref, k2_ref, v2_ref,
                t0_ref, t1_ref, t1f_ref, t2_ref, unperm_ref, y_ref,
                o1_ref, l1_ref, x16_ref, nat_ref):
    L, dh = DIL_BLOCK, DIL_DH
    S = q0_ref.shape[1]
    n_a = S // RES
    dil1 = DIL_PATTERNS[1][1]
    n_str = RES // dil1
    qa = L // n_str
    lane = lax.broadcasted_iota(jnp.int32, (L, dh), 1)
    zeros = jnp.zeros((L, dh), BF16)
    stages = []

    def g1_stage(r4):
        streams = [r4 + dil1 * s for s in range(n_str)]
        n_blocks = n_a // qa

        def gather(ref, lo, hi):
            return jnp.concatenate([ref[0, r, lo:hi, :] for r in streams], axis=0)

        def band(n):
            return (0, qa) if n == 0 else ((n - 1) * qa, (n + 1) * qa)

        def issue():
            return [_dot_nt(gather(q1_ref, n * qa, (n + 1) * qa), gather(k1_ref, *band(n)))
                    for n in range(n_blocks)]

        def consume(ss):
            parts = [_softmax_parts(s, t1f_ref[0][:, :L] if n == 0 else t1_ref[0])
                     for n, s in enumerate(ss)]
            nums = [_dot(p, gather(v1_ref, *band(n))) for n, (p, _, _) in enumerate(parts)]
            for n, (num, (_, mx, den)) in enumerate(zip(nums, parts)):
                o = num / den
                lse = jnp.broadcast_to(mx + jnp.log(den), (L, dh))
                for i, r in enumerate(streams):
                    o1_ref[r, n * qa:(n + 1) * qa, :] = o[i * qa:(i + 1) * qa]
                    l1_ref[r, n * qa:(n + 1) * qa, :] = lse[i * qa:(i + 1) * qa]

        return issue, consume

    stages += [g1_stage(r4) for r4 in range(dil1)]

    def g2_stage(pairs):
        def blockdiag(ref, r):
            return jnp.concatenate([jnp.concatenate([ref[0, r], zeros], axis=1),
                                    jnp.concatenate([zeros, ref[0, r + 1]], axis=1)], axis=0)

        def issue():
            return [_dot_nt(jnp.concatenate([q2_ref[0, r], q2_ref[0, r + 1]], axis=1),
                            blockdiag(k2_ref, r)) for r in pairs]

        def consume(ss):
            bias = t2_ref[0][:, L:]
            parts = [[_softmax_parts(s[:, i * L:(i + 1) * L], bias) for i in range(2)] for s in ss]
            outs = [_dot(jnp.concatenate([pp[0][0], pp[1][0]], axis=1), blockdiag(v2_ref, r))
                    for r, pp in zip(pairs, parts)]
            for r, pp, out in zip(pairs, parts, outs):
                for i in range(2):
                    _, mx, den = pp[i]
                    o2 = out[:, i * L:(i + 1) * L] / den
                    l2 = jnp.broadcast_to(mx + jnp.log(den), (L, dh))
                    o12, l12 = _lse_combine(o1_ref[r + i], l1_ref[r + i], o2, l2)
                    hi = l12.astype(BF16)
                    lo = (l12 - hi.astype(F32)).astype(BF16)
                    x16_ref[r + i] = jnp.concatenate(
                        [o12.astype(BF16), jnp.where(lane < dh // 2, hi, lo)], axis=1)

        return issue, consume

    stages += [g2_stage(range(r0, r0 + RES // 2, 2)) for r0 in (0, RES // 2)]

    def unpermute(_):
        for s in range(S // PERM_ROWS):
            xs = jnp.concatenate([x16_ref[r, s * RES:(s + 1) * RES, :] for r in range(RES)], axis=0)
            nat_ref[s * PERM_ROWS:(s + 1) * PERM_ROWS, :] = _dot(unperm_ref[...], xs)

    stages.append((lambda: None, unpermute))

    def g0_stage(blocks):
        def band(n):
            return (0, L) if n == 0 else ((n - 1) * L, (n + 1) * L)

        def issue():
            return [_dot_nt(q0_ref[0, n * L:(n + 1) * L, :], k0_ref[0, band(n)[0]:band(n)[1], :])
                    for n in blocks]

        def consume(ss):
            parts = [_softmax_parts(s, t0_ref[0][:, L:] if n == 0 else t0_ref[0])
                     for n, s in zip(blocks, ss)]
            nums = [_dot(p, v0_ref[0, band(n)[0]:band(n)[1], :])
                    for n, (p, _, _) in zip(blocks, parts)]
            for n, num, (_, mx, den) in zip(blocks, nums, parts):
                nat = nat_ref[n * L:(n + 1) * L, :]
                l12 = nat[:, dh:dh + 1] + nat[:, dh + dh // 2:dh + dh // 2 + 1]
                y, _ = _lse_combine(num / den, mx + jnp.log(den), nat[:, :dh], l12)
                y_ref[0, n * L:(n + 1) * L, :] = y.astype(y_ref.dtype)

        return issue, consume

    per_stage = 4
    stages += [g0_stage(range(n0, n0 + per_stage)) for n0 in range(0, S // L, per_stage)]
    _run_pipelined(stages)


def _dilated(pm, tiles, unperm):
    B, S, width = pm.shape
    L, dh, H = DIL_BLOCK, DIL_DH, DIL_HEADS
    n_a = S // RES
    c0 = _M_DIL0 // dh
    c1 = _MAIN_W // dh
    p16 = pm.reshape(B, RES, n_a, width)
    nat = lambda c: pl.BlockSpec((1, S, dh), lambda b, h: (b, 0, c + h))
    res = lambda c: pl.BlockSpec((1, RES, n_a, dh), lambda b, h: (b, 0, 0, c1 + c + h))
    tile = lambda t: pl.BlockSpec((1, L, 2 * L), lambda b, h: (t + h, 0, 0))
    return pl.pallas_call(
        _dil_kernel,
        grid=(B, H),
        in_specs=[nat(c0), nat(c0 + H), nat(c0 + 2 * H),
                  res(0), res(H), res(2 * H), res(3 * H), res(4 * H), res(5 * H),
                  tile(_TILE_G0), tile(_TILE_G1), tile(_TILE_G1_FIRST), tile(_TILE_G2),
                  pl.BlockSpec((PERM_ROWS, PERM_ROWS), lambda b, h: (0, 0))],
        out_specs=pl.BlockSpec((1, S, dh), lambda b, h: (b, 0, h)),
        out_shape=jax.ShapeDtypeStruct((B, S, H * dh), BF16),
        scratch_shapes=[pltpu.VMEM((RES, n_a, dh), F32), pltpu.VMEM((RES, n_a, dh), F32),
                        pltpu.VMEM((RES, n_a, 2 * dh), BF16), pltpu.VMEM((S, 2 * dh), F32)],
        compiler_params=_cparams(("parallel", "parallel")),
        name="dilated",
    )(pm, pm, pm, p16, p16, p16, p16, p16, p16, tiles, tiles, tiles, tiles, unperm)


def _merge_kernel(gate_ref, yr_ref, yd_ref, yl_ref, x_ref, wr_ref, wd_ref, wl_ref, wo_ref, out_ref):
    D = D_MODEL
    gates = jax.nn.sigmoid(gate_ref[...].astype(F32))
    merged = (gates[:, 0:D] * _dot(yr_ref[...], wr_ref[...])
              + gates[:, D:2 * D] * _dot(yd_ref[...], wd_ref[...])
              + gates[:, 2 * D:3 * D] * _dot(yl_ref[...], wl_ref[...]))
    out_ref[...] = x_ref[...] + _dot(merged.astype(BF16), wo_ref[...])


def _layer_spec(a, l):
    return pl.BlockSpec((None,) + a.shape[1:], lambda i: (l, 0, 0), pipeline_mode=pl.Buffered(1))


def _merge(pm2, y_ret, y_diff, y_dil, x2, wr, wd, wl, wo, l, tm=512):
    T, D = x2.shape
    row = lambda w: pl.BlockSpec((tm, w), lambda i: (i, 0))
    return pl.pallas_call(
        _merge_kernel,
        grid=(T // tm,),
        in_specs=[row(_GATE_W), row(D), row(D), row(y_dil.shape[1]), row(D),
                  _layer_spec(wr, l), _layer_spec(wd, l), _layer_spec(wl, l), _layer_spec(wo, l)],
        out_specs=row(D),
        out_shape=jax.ShapeDtypeStruct((T, D), F32),
        compiler_params=_cparams(("parallel",)),
        name="merge_out",
    )(pm2, y_ret, y_diff, y_dil, x2, wr, wd, wl, wo)


def _ffn_kernel(x_ref, g_ref, wg_ref, wu_ref, wd_ref, gf_ref, o_ref, *, final):
    x = x_ref[...]
    h = _rms(x, g_ref[...], RMS_EPS).astype(BF16)
    a = _dot(h, wg_ref[...])
    u = _dot(h, wu_ref[...])
    z = (a * jax.nn.sigmoid(a) * u).astype(BF16)
    y = x + _dot(z, wd_ref[...])
    if final:
        y = _rms(y, gf_ref[...], RMS_EPS)
    o_ref[...] = y


def _ffn(x2, g, wg, wu, wd, g_final, l, final, tm=512):
    T, D = x2.shape
    row = pl.BlockSpec((tm, D), lambda i: (i, 0))
    return pl.pallas_call(
        functools.partial(_ffn_kernel, final=final),
        grid=(T // tm,),
        in_specs=[row, _layer_spec(g, l), _layer_spec(wg, l), _layer_spec(wu, l),
                  _layer_spec(wd, l), _layer_spec(g_final, 0)],
        out_specs=row,
        out_shape=jax.ShapeDtypeStruct((T, D), F32),
        compiler_params=_cparams(("parallel",)),
        name="ffn_final" if final else "ffn",
    )(x2, g, wg, wu, wd, g_final)


def kernel(x, w_in, w_branch_ret, w_branch_diff, w_branch_dil, w_out, norm_mix, norm_ffn, ret_gn_gain, diff_lambda, diff_subln_gain, rel_bias, w_ffn_gate, w_ffn_up, w_ffn_down, norm_final):
    B, S, D = x.shape
    T = B * S
    tiles, far = _bias_tiles(rel_bias)
    ret_tables = _ret_tables(S)
    perm = _residue_perm()
    perm_fwd, perm_back = jnp.asarray(perm, BF16), jnp.asarray(perm.T, BF16)
    x2 = x.reshape(T, D)
    w_in, w_branch_ret, w_branch_diff, w_branch_dil, w_out, w_ffn_gate, w_ffn_up, w_ffn_down = (
        w.astype(BF16) for w in (w_in, w_branch_ret, w_branch_diff, w_branch_dil, w_out,
                                 w_ffn_gate, w_ffn_up, w_ffn_down))
    norm_mix, norm_ffn = norm_mix.reshape(DEPTH, 1, D), norm_ffn.reshape(DEPTH, 1, D)
    norm_final = norm_final.reshape(1, 1, D)
    for l in range(DEPTH):
        pm2 = _inproj(x2, norm_mix, perm_fwd, w_in, l, B)
        pm = pm2.reshape(B, S, -1)

        y_ret = _retention(pm, ret_tables, ret_gn_gain[l].reshape(1, -1))
        lam_init = 0.8 - 0.6 * math.exp(-0.3 * l)
        y_diff = _diff_attention(pm, tiles, far, diff_lambda[l], diff_subln_gain[l].reshape(1, -1),
                                 lam_init)
        y_dil = _dilated(pm, tiles, perm_back)

        x2 = _merge(pm2, y_ret.reshape(T, -1), y_diff.reshape(T, -1), y_dil.reshape(T, -1), x2,
                    w_branch_ret, w_branch_diff, w_branch_dil, w_out, l)
        x2 = _ffn(x2, norm_ffn, w_ffn_gate, w_ffn_up, w_ffn_down, norm_final, l,
                  final=(l == DEPTH - 1))
    return x2.reshape(B, S, D)
```

```python
import functools
import math

import numpy as np
import jax
import jax.numpy as jnp
from jax import lax
from jax.experimental import pallas as pl
from jax.experimental.pallas import tpu as pltpu

F32 = jnp.float32
BF16 = jnp.bfloat16

D_MODEL = 1024
DEPTH = 2
RET_HEADS, RET_DK, RET_DV, RET_CHUNK = 4, 128, 256, 128
ROPE_BASE = 10000.0
GN_EPS = 1e-5
DIFF_HEADS, DIFF_DH = 8, 64
DIFF_EPS = 1e-5
DIL_PATTERNS = ((128, 1), (512, 4), (2048, 16))
N_DIL = len(DIL_PATTERNS)
DIL_HEADS, DIL_DH, DIL_BLOCK = 4, 128, 128
REL_BUCKETS, REL_MAX_DIST = 32, 128
N_BIAS_HEADS = DIFF_HEADS + N_DIL * DIL_HEADS
FFN_HIDDEN = -(-8 * D_MODEL // (3 * 256)) * 256
RMS_EPS = 1e-6
NEG_INF = -1e30

_RET_W = 2 * RET_HEADS * RET_DK + 2 * RET_HEADS * RET_DV
_DIFF_W = 3 * DIFF_HEADS * 2 * DIFF_DH
_DIL_W = 3 * DIL_HEADS * DIL_DH
_GATE_W = 3 * D_MODEL
_OFF_DIFF = _RET_W
_OFF_DIL = _RET_W + _DIFF_W
_OFF_GATE = _OFF_DIL + N_DIL * _DIL_W
_MAIN_W = _GATE_W + _RET_W + _DIFF_W + _DIL_W
_M_RET = _GATE_W
_M_DIFF = _GATE_W + _RET_W
_M_DIL0 = _GATE_W + _RET_W + _DIFF_W

_LOG2E = math.log2(math.e)
_DIL_SCALE = DIL_DH ** -0.5
_DIFF_AHEAD = 3
_DIFF_HEADS_PER_STEP = 4
_DIL_SLOTS_PER_STEP = 2
_RET_HEADS_PER_STEP = 2

LANE = 128
VMEM_LIMIT = 56 * 1024 * 1024


def _cparams(sem):
    return pltpu.CompilerParams(dimension_semantics=sem, vmem_limit_bytes=VMEM_LIMIT)


def _rms(x, g, eps):
    return x * lax.rsqrt(jnp.mean(x * x, axis=-1, keepdims=True) + eps) * g


def _dot(a, b):
    return jnp.dot(a, b, preferred_element_type=F32)


def _dot_nt(a, b):
    return lax.dot_general(a, b, (((1,), (1,)), ((), ())), preferred_element_type=F32)


def _dot_tn(a, b):
    return lax.dot_general(a, b, (((0,), (0,)), ((), ())), preferred_element_type=F32)


RES = 16
PERM_ROWS = RES * RES


def _residue_perm():
    p = np.zeros((PERM_ROWS, PERM_ROWS), np.float32)
    a, r = np.meshgrid(np.arange(RES), np.arange(RES), indexing="ij")
    p[(r * RES + a).ravel(), (a * RES + r).ravel()] = 1.0
    return p


def _inproj_kernel(x_ref, g_ref, perm_ref, w_ref, o_ref, h_ref, hp_ref, *, n_main, row_chunk):
    j = pl.program_id(1)
    S = x_ref.shape[0]
    per_res = S // RES

    @pl.when(j == 0)
    def _():
        for s in range(S // PERM_ROWS):
            rows = pl.ds(s * PERM_ROWS, PERM_ROWS)
            h = _rms(x_ref[rows, :], g_ref[...], RMS_EPS).astype(BF16)
            h_ref[rows, :] = h
            hs = _dot(perm_ref[...], h).astype(BF16)
            for r in range(RES):
                hp_ref[pl.ds(r * per_res + s * RES, RES), :] = hs[r * RES:(r + 1) * RES]

    def project(src_ref):
        for c in range(S // row_chunk):
            rows = pl.ds(c * row_chunk, row_chunk)
            o_ref[rows, :] = _dot(src_ref[rows, :], w_ref[...]).astype(o_ref.dtype)

    @pl.when(j < n_main)
    def _():
        project(h_ref)

    @pl.when(j >= n_main)
    def _():
        project(hp_ref)


def _inproj(x2, g, perm, w, l, B, tn=1536, row_chunk=1024):
    T, D = x2.shape
    S = T // B
    n_main = _MAIN_W // tn
    n_tiles = w.shape[2] // tn
    first = _OFF_GATE // tn
    return pl.pallas_call(
        functools.partial(_inproj_kernel, n_main=n_main, row_chunk=row_chunk),
        grid=(B, n_tiles),
        in_specs=[pl.BlockSpec((S, D), lambda i, j: (i, 0)),
                  pl.BlockSpec((None, 1, D), lambda i, j: (l, 0, 0)),
                  pl.BlockSpec((PERM_ROWS, PERM_ROWS), lambda i, j: (0, 0)),
                  pl.BlockSpec((None, D, tn), lambda i, j: (l, 0, (j + first) % n_tiles))],
        out_specs=pl.BlockSpec((S, tn), lambda i, j: (i, j)),
        out_shape=jax.ShapeDtypeStruct((T, w.shape[2]), BF16),
        scratch_shapes=[pltpu.VMEM((S, D), BF16), pltpu.VMEM((S, D), BF16)],
        compiler_params=_cparams(("parallel", "arbitrary")),
        name="inproj",
    )(x2, g, perm, w)


def _bucket_patterns():
    L = DIL_BLOCK
    i = np.arange(L)[:, None]
    j = np.arange(2 * L)[None, :]
    m = i + L - j
    n_str = RES // DIL_PATTERNS[1][1]
    qa = L // n_str
    u_q = 4 * (i % qa) + i // qa
    m_g1 = u_q + L - (4 * (j % (2 * qa)) + j // (2 * qa))
    m_g1_first = np.where(j < L, u_q - (4 * (j % qa) + j // qa), -1)

    def bucket(dist):
        n = np.maximum(dist, 0)
        exact = REL_BUCKETS // 2
        log_ratio = (np.log(np.maximum(n, exact).astype(np.float32) / np.float32(exact))
                     / np.float32(math.log(REL_MAX_DIST / exact))).astype(np.float32)
        large = np.minimum(exact + (log_ratio * np.float32(REL_BUCKETS - exact)).astype(np.int32),
                           REL_BUCKETS - 1)
        return np.where(n < exact, n, large).astype(np.int32)

    def windowed(mm, gi):
        window, dil = DIL_PATTERNS[gi]
        return np.where((mm >= 0) & (mm <= window // dil), bucket(mm * dil), -1)

    pats = [np.where(m >= 0, bucket(m), -1), windowed(m, 0), windowed(m_g1, 1),
            windowed(m_g1_first, 1), windowed(m, 2)]
    return np.stack(pats).astype(np.int32)


_TILE_HEADS = (list(range(DIFF_HEADS + 2 * DIL_HEADS))
               + list(range(DIFF_HEADS + DIL_HEADS, DIFF_HEADS + 3 * DIL_HEADS)))
_TILE_PATS = [0] * DIFF_HEADS + [1] * DIL_HEADS + [2] * DIL_HEADS + [3] * DIL_HEADS + [4] * DIL_HEADS
_TILE_G0, _TILE_G1, _TILE_G1_FIRST, _TILE_G2 = (DIFF_HEADS + k * DIL_HEADS for k in range(4))


def _bias_kernel(head_ref, patid_ref, tbl_ref, pat_ref, tile_ref, far_ref):
    t = pl.program_id(0)
    h = head_ref[t]
    pat = pat_ref[0]
    mult = jnp.where(patid_ref[t] == 0, 1.0, 1.0 / _DIL_SCALE).astype(F32)
    tile = jnp.full(pat.shape, NEG_INF, F32)
    for b in range(REL_BUCKETS):
        tile = jnp.where(pat == b, tbl_ref[h, b] * mult, tile)
    tile_ref[0] = tile
    far_ref[0] = jnp.full(far_ref.shape[1:], tbl_ref[h, REL_BUCKETS - 1], F32)


def _bias_tiles(rel_bias):
    pats = jnp.asarray(_bucket_patterns())
    tbl = rel_bias.T
    n_tiles = len(_TILE_HEADS)
    shape = (1, DIL_BLOCK, 2 * DIL_BLOCK)
    return pl.pallas_call(
        _bias_kernel,
        grid_spec=pltpu.PrefetchScalarGridSpec(
            num_scalar_prefetch=2,
            grid=(n_tiles,),
            in_specs=[pl.BlockSpec(memory_space=pltpu.SMEM),
                      pl.BlockSpec(shape, lambda t, heads, patids: (patids[t], 0, 0))],
            out_specs=[pl.BlockSpec(shape, lambda t, heads, patids: (t, 0, 0)),
                       pl.BlockSpec((1, 8, LANE), lambda t, heads, patids: (t, 0, 0))]),
        out_shape=[jax.ShapeDtypeStruct((n_tiles,) + shape[1:], F32),
                   jax.ShapeDtypeStruct((n_tiles, 8, LANE), F32)],
        compiler_params=_cparams(("arbitrary",)),
        name="bias_tiles",
    )(jnp.asarray(_TILE_HEADS, jnp.int32), jnp.asarray(_TILE_PATS, jnp.int32), tbl, pats)


def _ret_kernel(*refs):
    for hh in range(_RET_HEADS_PER_STEP):
        _ret_head(hh, *refs)


def _ret_head(hh, q_ref, k_ref, v_ref, g_ref, cos_ref, sin_ref, dec_ref, kend_ref, qst_ref,
              cdec_ref, gain_ref, o_ref):
    c = RET_CHUNK
    n_chunks = q_ref.shape[1] // c
    ck = slice(hh * RET_DK, (hh + 1) * RET_DK)
    cv = slice(hh * RET_DV, (hh + 1) * RET_DV)
    scores, q_cross, kvs = [], [], []
    for n in range(n_chunks):
        rows = pl.ds(n * c, c)
        cos = cos_ref[rows, :]
        sin = sin_ref[rows, :]
        q = q_ref[0, rows, ck].astype(F32)
        k = k_ref[0, rows, ck].astype(F32)
        qr = q * cos + pltpu.roll(q, RET_DK // 2, 1) * sin
        kr = (k * cos + pltpu.roll(k, RET_DK // 2, 1) * sin) * (RET_DK ** -0.5)
        scores.append(_dot_nt(qr.astype(BF16), kr.astype(BF16)))
        q_cross.append((qr * qst_ref[hh]).astype(BF16))
        if n < n_chunks - 1:
            kvs.append(_dot_tn((kr * kend_ref[hh]).astype(BF16), v_ref[0, rows, cv]))

    states = [None]
    st = None
    for n in range(n_chunks - 1):
        st = kvs[n] if st is None else st * cdec_ref[hh] + kvs[n]
        states.append(st.astype(BF16))

    for n in range(n_chunks):
        rows = pl.ds(n * c, c)
        y = _dot((scores[n] * dec_ref[hh]).astype(BF16), v_ref[0, rows, cv])
        if states[n] is not None:
            y = y + _dot(q_cross[n], states[n])
        mu = jnp.mean(y, axis=-1, keepdims=True)
        yc = y - mu
        var = jnp.mean(yc * yc, axis=-1, keepdims=True)
        yn = yc * lax.rsqrt(var + GN_EPS) * gain_ref[:, cv]
        g = g_ref[0, rows, cv]
        o_ref[0, rows, cv] = ((g * jax.nn.sigmoid(g)).astype(F32) * yn).astype(o_ref.dtype)


def _ret_tables(S):
    H, dk, c = RET_HEADS, RET_DK, RET_CHUNK
    half = dk // 2
    pos = np.arange(S, dtype=np.float64)
    inv = ROPE_BASE ** (-np.arange(half, dtype=np.float64) / half)
    ang = pos[:, None] * inv[None, :]
    cos, sin = np.cos(ang), np.sin(ang)
    cos2 = np.concatenate([cos, cos], axis=-1)
    sin2 = np.concatenate([-sin, sin], axis=-1)
    log_gamma = np.log1p(-np.exp2(-5.0 - np.arange(H, dtype=np.float64)))
    i = np.arange(c, dtype=np.float64)
    rel = i[:, None] - i[None, :]
    decay = np.where(rel >= 0, np.exp(log_gamma[:, None, None] * np.maximum(rel, 0.0)), 0.0)
    k_to_end = np.exp((c - 1.0 - i)[None, :] * log_gamma[:, None])
    q_from_start = np.exp((i + 1.0)[None, :] * log_gamma[:, None])
    kend = np.broadcast_to(k_to_end[:, :, None], (H, c, dk))
    qst = np.broadcast_to(q_from_start[:, :, None], (H, c, dk))
    cdec = np.broadcast_to(np.exp(c * log_gamma)[:, None, None], (H, 1, RET_DV))
    return tuple(jnp.asarray(t, F32) for t in (cos2, sin2, decay, kend, qst, cdec))


def _retention(pm, tables, gain):
    B, S, _ = pm.shape
    H, dk, dv, c = RET_HEADS, RET_DK, RET_DV, RET_CHUNK
    cos2, sin2, decay, kend, qst, cdec = tables
    n = _RET_HEADS_PER_STEP
    qb = _M_RET // (n * dk)
    kb = qb + H // n
    vb = (_M_RET + 2 * H * dk) // (n * dv)
    gb = vb + H // n
    const2 = lambda b, h: (0, 0)
    head3 = lambda b, h: (h, 0, 0)
    return pl.pallas_call(
        _ret_kernel,
        grid=(B, H // n),
        in_specs=[pl.BlockSpec((1, S, n * dk), lambda b, h: (b, 0, qb + h)),
                  pl.BlockSpec((1, S, n * dk), lambda b, h: (b, 0, kb + h)),
                  pl.BlockSpec((1, S, n * dv), lambda b, h: (b, 0, vb + h)),
                  pl.BlockSpec((1, S, n * dv), lambda b, h: (b, 0, gb + h)),
                  pl.BlockSpec((S, dk), const2),
                  pl.BlockSpec((S, dk), const2),
                  pl.BlockSpec((n, c, c), head3),
                  pl.BlockSpec((n, c, dk), head3),
                  pl.BlockSpec((n, c, dk), head3),
                  pl.BlockSpec((n, 1, dv), head3),
                  pl.BlockSpec((1, n * dv), lambda b, h: (0, h))],
        out_specs=pl.BlockSpec((1, S, n * dv), lambda b, h: (b, 0, h)),
        out_shape=jax.ShapeDtypeStruct((B, S, H * dv), BF16),
        compiler_params=_cparams(("parallel", "parallel")),
        name="retention",
    )(pm, pm, pm, pm, cos2, sin2, decay, kend, qst, cdec, gain)


def _diff_kernel(q_ref, k_ref, v_ref, bias_ref, far_ref, lam_ref, gain_ref, o_ref, vv_ref, *,
                 lam_init):
    for hh in range(_DIFF_HEADS_PER_STEP):
        _diff_head(q_ref, k_ref, v_ref, bias_ref, far_ref, lam_ref, gain_ref, o_ref, vv_ref, hh,
                   lam_init)


def _diff_head(q_ref, k_ref, v_ref, bias_ref, far_ref, lam_ref, gain_ref, o_ref, vv_ref, hh,
               lam_init):
    S = q_ref.shape[1]
    QB = DIL_BLOCK
    dv = 2 * DIFF_DH
    cs = slice(hh * dv, (hh + 1) * dv)
    vv_ref[hh, :, :dv] = v_ref[0, :, cs]
    vv_ref[hh, :, dv:] = jnp.ones((S, dv), BF16)
    first = lax.broadcasted_iota(jnp.int32, (QB, 2 * DIFF_DH), 1) < DIFF_DH
    lp = lam_ref[...]
    lam = (jnp.exp(jnp.sum(lp[0:1] * lp[1:2], axis=-1, keepdims=True))
           - jnp.exp(jnp.sum(lp[2:3] * lp[3:4], axis=-1, keepdims=True)) + lam_init)
    bias = (bias_ref[hh] - far_ref[hh][0:1, 0:1]) * _LOG2E
    bias2 = jnp.concatenate([bias, bias], axis=0)
    gain = gain_ref[...] * (1.0 - lam_init)
    zero = jnp.zeros((), BF16)
    n_q = S // QB

    def bounds(qi):
        return max(qi - 1, 0) * QB, (qi + 1) * QB

    def scores(qi):
        qb = (q_ref[0, qi * QB:(qi + 1) * QB, cs].astype(F32)
              * (DIFF_DH ** -0.5 * _LOG2E)).astype(BF16)
        qq = jnp.concatenate([jnp.where(first, qb, zero), jnp.where(first, zero, qb)], axis=0)
        lo, hi = bounds(qi)
        sb = _dot_nt(qq, k_ref[0, lo:hi, cs]) + (bias2 if qi > 0 else bias2[:, QB:])
        sf = _dot_nt(qq, k_ref[0, 0:lo, cs]) if lo > 0 else None
        return sb, sf

    def finish(qi, sb, sf):
        lo, hi = bounds(qi)
        m = jnp.max(sb, axis=-1, keepdims=True)
        if sf is not None:
            m = jnp.maximum(m, jnp.max(sf, axis=-1, keepdims=True))
        acc = _dot(jnp.exp2((sb - m).astype(BF16)), vv_ref[hh, lo:hi, :])
        if sf is not None:
            acc = acc + _dot(jnp.exp2((sf - m).astype(BF16)), vv_ref[hh, 0:lo, :])
        r = acc[:, :dv] / acc[:, dv:]
        o = r[:QB] - lam * r[QB:]
        o = o * lax.rsqrt(jnp.mean(o * o, axis=-1, keepdims=True) + DIFF_EPS) * gain
        o_ref[0, qi * QB:(qi + 1) * QB, cs] = o.astype(o_ref.dtype)

    pending = {}
    for step in range(n_q + _DIFF_AHEAD):
        if step < n_q:
            pending[step] = scores(step)
        if step >= _DIFF_AHEAD:
            finish(step - _DIFF_AHEAD, *pending.pop(step - _DIFF_AHEAD))


def _diff_attention(pm, tiles, far, lam_params, gain, lam_init):
    B, S, _ = pm.shape
    H, hw = DIFF_HEADS, 2 * DIFF_DH
    n = _DIFF_HEADS_PER_STEP
    w = n * hw
    qb = _M_DIFF // w
    kb, vb = qb + H // n, qb + 2 * (H // n)
    return pl.pallas_call(
        functools.partial(_diff_kernel, lam_init=lam_init),
        grid=(B, H // n),
        in_specs=[pl.BlockSpec((1, S, w), lambda b, h: (b, 0, qb + h)),
                  pl.BlockSpec((1, S, w), lambda b, h: (b, 0, kb + h)),
                  pl.BlockSpec((1, S, w), lambda b, h: (b, 0, vb + h)),
                  pl.BlockSpec((n, DIL_BLOCK, 2 * DIL_BLOCK), lambda b, h: (h, 0, 0)),
                  pl.BlockSpec((n, 8, LANE), lambda b, h: (h, 0, 0)),
                  pl.BlockSpec((4, DIFF_DH), lambda b, h: (0, 0)),
                  pl.BlockSpec((1, hw), lambda b, h: (0, 0))],
        out_specs=pl.BlockSpec((1, S, w), lambda b, h: (b, 0, h)),
        out_shape=jax.ShapeDtypeStruct((B, S, H * hw), BF16),
        scratch_shapes=[pltpu.VMEM((n, S, 2 * hw), BF16)],
        compiler_params=_cparams(("parallel", "parallel")),
        name="diff_attention",
    )(pm, pm, pm, tiles, far, lam_params, gain)


def _run_pipelined(stages):
    pending = None
    for issue, consume in list(stages) + [(None, None)]:
        issued = issue() if issue is not None else None
        if pending is not None:
            pending[1](pending[0])
        pending = (issued, consume) if consume is not None else None


def _softmax_parts(s, bias):
    s = s + bias
    mx = jnp.max(s, axis=-1, keepdims=True)
    p = jnp.exp2((s - mx) * (_DIL_SCALE * _LOG2E))
    return p.astype(BF16), mx * _DIL_SCALE, jnp.sum(p, axis=-1, keepdims=True)


def _lse_combine(o_a, l_a, o_b, l_b):
    mx = jnp.maximum(l_a, l_b)
    w_a, w_b = jnp.exp(l_a - mx), jnp.exp(l_b - mx)
    den = w_a + w_b
    return (w_a * o_a + w_b * o_b) / den, mx + jnp.log(den)


def _dil_kernel(*refs):
    for hs in range(_DIL_SLOTS_PER_STEP):
        _dil_slot(hs, *refs)


def _dil_slot(hs, q0_ref, k0_ref, v0_ref, q1_ref, k1_ref, v1_ref, q2_ref, k2_ref, v2_ref,
              t0_ref, t1_ref, t1f_ref, t2_ref, unperm_ref, y_ref,
              o1_ref, l1_ref, x16_ref, nat_ref):
    L, dh = DIL_BLOCK, DIL_DH
    S = q0_ref.shape[1]
    n_a = S // RES
    dil1 = DIL_PATTERNS[1][1]
    n_str = RES // dil1
    qa = L // n_str
    cs = slice(hs * dh, (hs + 1) * dh)
    lane = lax.broadcasted_iota(jnp.int32, (L, dh), 1)
    zeros = jnp.zeros((L, dh), BF16)
    stages = []

    def g1_stage(r4):
        streams = [r4 + dil1 * s for s in range(n_str)]
        n_blocks = n_a // qa

        def gather(ref, lo, hi):
            return jnp.concatenate([ref[0, r, lo:hi, cs] for r in streams], axis=0)

        def band(n):
            return (0, qa) if n == 0 else ((n - 1) * qa, (n + 1) * qa)

        def issue():
            return [_dot_nt(gather(q1_ref, n * qa, (n + 1) * qa), gather(k1_ref, *band(n)))
                    for n in range(n_blocks)]

        def consume(ss):
            parts = [_softmax_parts(s, t1f_ref[hs][:, :L] if n == 0 else t1_ref[hs])
                     for n, s in enumerate(ss)]
            nums = [_dot(p, gather(v1_ref, *band(n))) for n, (p, _, _) in enumerate(parts)]
            for n, (num, (_, mx, den)) in enumerate(zip(nums, parts)):
                o = num / den
                lse = jnp.broadcast_to(mx + jnp.log(den), (L, dh))
                for i, r in enumerate(streams):
                    o1_ref[hs, r, n * qa:(n + 1) * qa, :] = o[i * qa:(i + 1) * qa]
                    l1_ref[hs, r, n * qa:(n + 1) * qa, :] = lse[i * qa:(i + 1) * qa]

        return issue, consume

    stages += [g1_stage(r4) for r4 in range(dil1)]

    def g2_stage(pairs):
        def blockdiag(ref, r):
            return jnp.concatenate([jnp.concatenate([ref[0, r, :, cs], zeros], axis=1),
                                    jnp.concatenate([zeros, ref[0, r + 1, :, cs]], axis=1)], axis=0)

        def issue():
            return [_dot_nt(jnp.concatenate([q2_ref[0, r, :, cs], q2_ref[0, r + 1, :, cs]], axis=1),
                            blockdiag(k2_ref, r)) for r in pairs]

        def consume(ss):
            bias = t2_ref[hs][:, L:]
            parts = [[_softmax_parts(s[:, i * L:(i + 1) * L], bias) for i in range(2)] for s in ss]
            outs = [_dot(jnp.concatenate([pp[0][0], pp[1][0]], axis=1), blockdiag(v2_ref, r))
                    for r, pp in zip(pairs, parts)]
            for r, pp, out in zip(pairs, parts, outs):
                for i in range(2):
                    _, mx, den = pp[i]
                    o2 = out[:, i * L:(i + 1) * L] / den
                    l2 = jnp.broadcast_to(mx + jnp.log(den), (L, dh))
                    o12, l12 = _lse_combine(o1_ref[hs, r + i], l1_ref[hs, r + i], o2, l2)
                    hi = l12.astype(BF16)
                    lo = (l12 - hi.astype(F32)).astype(BF16)
                    x16_ref[hs, r + i] = jnp.concatenate(
                        [o12.astype(BF16), jnp.where(lane < dh // 2, hi, lo)], axis=1)

        return issue, consume

    stages += [g2_stage(range(r0, r0 + RES // 2, 2)) for r0 in (0, RES // 2)]

    def unpermute(_):
        for s in range(S // PERM_ROWS):
            xs = jnp.concatenate([x16_ref[hs, r, s * RES:(s + 1) * RES, :] for r in range(RES)],
                                 axis=0)
            nat_ref[hs, s * PERM_ROWS:(s + 1) * PERM_ROWS, :] = _dot(unperm_ref[...], xs)

    stages.append((lambda: None, unpermute))

    def g0_stage(blocks):
        def band(n):
            return (0, L) if n == 0 else ((n - 1) * L, (n + 1) * L)

        def issue():
            return [_dot_nt(q0_ref[0, n * L:(n + 1) * L, cs], k0_ref[0, band(n)[0]:band(n)[1], cs])
                    for n in blocks]

        def consume(ss):
            parts = [_softmax_parts(s, t0_ref[hs][:, L:] if n == 0 else t0_ref[hs])
                     for n, s in zip(blocks, ss)]
            nums = [_dot(p, v0_ref[0, band(n)[0]:band(n)[1], cs])
                    for n, (p, _, _) in zip(blocks, parts)]
            for n, num, (_, mx, den) in zip(blocks, nums, parts):
                nat = nat_ref[hs, n * L:(n + 1) * L, :]
                l12 = nat[:, dh:dh + 1] + nat[:, dh + dh // 2:dh + dh // 2 + 1]
                y, _ = _lse_combine(num / den, mx + jnp.log(den), nat[:, :dh], l12)
                y_ref[0, n * L:(n + 1) * L, cs] = y.astype(y_ref.dtype)

        return issue, consume

    per_stage = 4
    stages += [g0_stage(range(n0, n0 + per_stage)) for n0 in range(0, S // L, per_stage)]
    _run_pipelined(stages)


def _dilated(pm, tiles, unperm):
    B, S, width = pm.shape
    L, dh, H = DIL_BLOCK, DIL_DH, DIL_HEADS
    n_a = S // RES
    c0 = _M_DIL0 // dh
    c1 = _MAIN_W // dh
    p16 = pm.reshape(B, RES, n_a, width)
    n = _DIL_SLOTS_PER_STEP
    w = n * dh
    nat = lambda c: pl.BlockSpec((1, S, w), lambda b, h: (b, 0, c // n + h))
    res = lambda c: pl.BlockSpec((1, RES, n_a, w), lambda b, h: (b, 0, 0, (c1 + c) // n + h))
    tile = lambda t: pl.BlockSpec((n, L, 2 * L), lambda b, h: (t // n + h, 0, 0))
    return pl.pallas_call(
        _dil_kernel,
        grid=(B, H // n),
        in_specs=[nat(c0), nat(c0 + H), nat(c0 + 2 * H),
                  res(0), res(H), res(2 * H), res(3 * H), res(4 * H), res(5 * H),
                  tile(_TILE_G0), tile(_TILE_G1), tile(_TILE_G1_FIRST), tile(_TILE_G2),
                  pl.BlockSpec((PERM_ROWS, PERM_ROWS), lambda b, h: (0, 0))],
        out_specs=pl.BlockSpec((1, S, w), lambda b, h: (b, 0, h)),
        out_shape=jax.ShapeDtypeStruct((B, S, H * dh), BF16),
        scratch_shapes=[pltpu.VMEM((n, RES, n_a, dh), F32), pltpu.VMEM((n, RES, n_a, dh), F32),
                        pltpu.VMEM((n, RES, n_a, 2 * dh), BF16), pltpu.VMEM((n, S, 2 * dh), F32)],
        compiler_params=_cparams(("parallel", "parallel")),
        name="dilated",
    )(pm, pm, pm, p16, p16, p16, p16, p16, p16, tiles, tiles, tiles, tiles, unperm)


def _merge_kernel(gate_ref, yr_ref, yd_ref, yl_ref, x_ref, wr_ref, wd_ref, wl_ref, wo_ref, out_ref):
    D = D_MODEL
    gates = jax.nn.sigmoid(gate_ref[...].astype(F32))
    merged = (gates[:, 0:D] * _dot(yr_ref[...], wr_ref[...])
              + gates[:, D:2 * D] * _dot(yd_ref[...], wd_ref[...])
              + gates[:, 2 * D:3 * D] * _dot(yl_ref[...], wl_ref[...]))
    out_ref[...] = x_ref[...] + _dot(merged.astype(BF16), wo_ref[...])


def _layer_spec(a, l):
    return pl.BlockSpec((None,) + a.shape[1:], lambda i: (l, 0, 0), pipeline_mode=pl.Buffered(1))


def _merge(pm2, y_ret, y_diff, y_dil, x2, wr, wd, wl, wo, l, tm=1024):
    T, D = x2.shape
    row = lambda w: pl.BlockSpec((tm, w), lambda i: (i, 0))
    return pl.pallas_call(
        _merge_kernel,
        grid=(T // tm,),
        in_specs=[row(_GATE_W), row(D), row(D), row(y_dil.shape[1]), row(D),
                  _layer_spec(wr, l), _layer_spec(wd, l), _layer_spec(wl, l), _layer_spec(wo, l)],
        out_specs=row(D),
        out_shape=jax.ShapeDtypeStruct((T, D), F32),
        compiler_params=_cparams(("parallel",)),
        name="merge_out",
    )(pm2, y_ret, y_diff, y_dil, x2, wr, wd, wl, wo)


def _ffn_kernel(x_ref, g_ref, wg_ref, wu_ref, wd_ref, gf_ref, o_ref, *, final):
    x = x_ref[...]
    h = _rms(x, g_ref[...], RMS_EPS).astype(BF16)
    a = _dot(h, wg_ref[...])
    u = _dot(h, wu_ref[...])
    z = (a * jax.nn.sigmoid(a) * u).astype(BF16)
    y = x + _dot(z, wd_ref[...])
    if final:
        y = _rms(y, gf_ref[...], RMS_EPS)
    o_ref[...] = y


def _ffn(x2, g, wg, wu, wd, g_final, l, final, tm=1024):
    T, D = x2.shape
    row = pl.BlockSpec((tm, D), lambda i: (i, 0))
    return pl.pallas_call(
        functools.partial(_ffn_kernel, final=final),
        grid=(T // tm,),
        in_specs=[row, _layer_spec(g, l), _layer_spec(wg, l), _layer_spec(wu, l),
                  _layer_spec(wd, l), _layer_spec(g_final, 0)],
        out_specs=row,
        out_shape=jax.ShapeDtypeStruct((T, D), F32),
        compiler_params=_cparams(("parallel",)),
        name="ffn_final" if final else "ffn",
    )(x2, g, wg, wu, wd, g_final)


def kernel(x, w_in, w_branch_ret, w_branch_diff, w_branch_dil, w_out, norm_mix, norm_ffn, ret_gn_gain, diff_lambda, diff_subln_gain, rel_bias, w_ffn_gate, w_ffn_up, w_ffn_down, norm_final):
    B, S, D = x.shape
    T = B * S
    tiles, far = _bias_tiles(rel_bias)
    ret_tables = _ret_tables(S)
    perm = _residue_perm()
    perm_fwd, perm_back = jnp.asarray(perm, BF16), jnp.asarray(perm.T, BF16)
    x2 = x.reshape(T, D)
    w_in, w_branch_ret, w_branch_diff, w_branch_dil, w_out, w_ffn_gate, w_ffn_up, w_ffn_down = (
        w.astype(BF16) for w in (w_in, w_branch_ret, w_branch_diff, w_branch_dil, w_out,
                                 w_ffn_gate, w_ffn_up, w_ffn_down))
    norm_mix, norm_ffn = norm_mix.reshape(DEPTH, 1, D), norm_ffn.reshape(DEPTH, 1, D)
    norm_final = norm_final.reshape(1, 1, D)
    for l in range(DEPTH):
        pm2 = _inproj(x2, norm_mix, perm_fwd, w_in, l, B)
        pm = pm2.reshape(B, S, -1)

        y_ret = _retention(pm, ret_tables, ret_gn_gain[l].reshape(1, -1))
        lam_init = 0.8 - 0.6 * math.exp(-0.3 * l)
        y_diff = _diff_attention(pm, tiles, far, diff_lambda[l], diff_subln_gain[l].reshape(1, -1),
                                 lam_init)
        y_dil = _dilated(pm, tiles, perm_back)

        x2 = _merge(pm2, y_ret.reshape(T, -1), y_diff.reshape(T, -1), y_dil.reshape(T, -1), x2,
                    w_branch_ret, w_branch_diff, w_branch_dil, w_out, l)
        x2 = _ffn(x2, norm_ffn, w_ffn_gate, w_ffn_up, w_ffn_down, norm_final, l,
                  final=(l == DEPTH - 1))
    return x2.reshape(B, S, D)
```

```python
import functools
import math

import numpy as np
import jax
import jax.numpy as jnp
from jax import lax
from jax.experimental import pallas as pl
from jax.experimental.pallas import tpu as pltpu

F32 = jnp.float32
BF16 = jnp.bfloat16

D_MODEL = 1024
DEPTH = 2
RET_HEADS, RET_DK, RET_DV, RET_CHUNK = 4, 128, 256, 128
ROPE_BASE = 10000.0
GN_EPS = 1e-5
DIFF_HEADS, DIFF_DH = 8, 64
DIFF_EPS = 1e-5
DIL_PATTERNS = ((128, 1), (512, 4), (2048, 16))
N_DIL = len(DIL_PATTERNS)
DIL_HEADS, DIL_DH, DIL_BLOCK = 4, 128, 128
REL_BUCKETS, REL_MAX_DIST = 32, 128
N_BIAS_HEADS = DIFF_HEADS + N_DIL * DIL_HEADS
FFN_HIDDEN = -(-8 * D_MODEL // (3 * 256)) * 256
RMS_EPS = 1e-6
NEG_INF = -1e30

_RET_W = 2 * RET_HEADS * RET_DK + 2 * RET_HEADS * RET_DV
_DIFF_W = 3 * DIFF_HEADS * 2 * DIFF_DH
_DIL_W = 3 * DIL_HEADS * DIL_DH
_GATE_W = 3 * D_MODEL
_OFF_DIFF = _RET_W
_OFF_DIL = _RET_W + _DIFF_W
_OFF_GATE = _OFF_DIL + N_DIL * _DIL_W
_MAIN_W = _GATE_W + _RET_W + _DIFF_W + _DIL_W
_M_RET = _GATE_W
_M_DIFF = _GATE_W + _RET_W
_M_DIL0 = _GATE_W + _RET_W + _DIFF_W

_LOG2E = math.log2(math.e)
_DIL_SCALE = DIL_DH ** -0.5
_DIFF_AHEAD = 3
_DIFF_HEADS_PER_STEP = 4
_DIL_SLOTS_PER_STEP = 2
_RET_HEADS_PER_STEP = 2

LANE = 128
VMEM_LIMIT = 56 * 1024 * 1024


def _cparams(sem):
    return pltpu.CompilerParams(dimension_semantics=sem, vmem_limit_bytes=VMEM_LIMIT)


def _rms(x, g, eps):
    return x * lax.rsqrt(jnp.mean(x * x, axis=-1, keepdims=True) + eps) * g


def _dot(a, b):
    return jnp.dot(a, b, preferred_element_type=F32)


def _dot_nt(a, b):
    return lax.dot_general(a, b, (((1,), (1,)), ((), ())), preferred_element_type=F32)


def _dot_tn(a, b):
    return lax.dot_general(a, b, (((0,), (0,)), ((), ())), preferred_element_type=F32)


RES = 16
PERM_ROWS = RES * RES


def _residue_perm():
    p = np.zeros((PERM_ROWS, PERM_ROWS), np.float32)
    a, r = np.meshgrid(np.arange(RES), np.arange(RES), indexing="ij")
    p[(r * RES + a).ravel(), (a * RES + r).ravel()] = 1.0
    return p


def _inproj_kernel(x_ref, g_ref, perm_ref, w_ref, o_ref, h_ref, hp_ref, *, n_main, row_chunk):
    j = pl.program_id(1)
    S = x_ref.shape[0]
    per_res = S // RES

    @pl.when(j == 0)
    def _():
        for s in range(S // PERM_ROWS):
            rows = pl.ds(s * PERM_ROWS, PERM_ROWS)
            h = _rms(x_ref[rows, :], g_ref[...], RMS_EPS).astype(BF16)
            h_ref[rows, :] = h
            hs = _dot(perm_ref[...], h).astype(BF16)
            for r in range(RES):
                hp_ref[pl.ds(r * per_res + s * RES, RES), :] = hs[r * RES:(r + 1) * RES]

    def project(src_ref):
        w = w_ref[...].astype(BF16)
        for c in range(S // row_chunk):
            rows = pl.ds(c * row_chunk, row_chunk)
            o_ref[rows, :] = _dot(src_ref[rows, :], w).astype(o_ref.dtype)

    @pl.when(j < n_main)
    def _():
        project(h_ref)

    @pl.when(j >= n_main)
    def _():
        project(hp_ref)


def _inproj(x2, g, perm, w, l, B, tn=1536, row_chunk=1024):
    T, D = x2.shape
    S = T // B
    n_main = _MAIN_W // tn
    n_tiles = w.shape[2] // tn
    first = _OFF_GATE // tn
    return pl.pallas_call(
        functools.partial(_inproj_kernel, n_main=n_main, row_chunk=row_chunk),
        grid=(B, n_tiles),
        in_specs=[pl.BlockSpec((S, D), lambda i, j: (i, 0)),
                  pl.BlockSpec((None, 1, D), lambda i, j: (l, 0, 0)),
                  pl.BlockSpec((PERM_ROWS, PERM_ROWS), lambda i, j: (0, 0)),
                  pl.BlockSpec((None, D, tn), lambda i, j: (l, 0, (j + first) % n_tiles))],
        out_specs=pl.BlockSpec((S, tn), lambda i, j: (i, j)),
        out_shape=jax.ShapeDtypeStruct((T, w.shape[2]), BF16),
        scratch_shapes=[pltpu.VMEM((S, D), BF16), pltpu.VMEM((S, D), BF16)],
        compiler_params=_cparams(("parallel", "arbitrary")),
        name="inproj",
    )(x2, g, perm, w)


def _bucket_patterns():
    L = DIL_BLOCK
    i = np.arange(L)[:, None]
    j = np.arange(2 * L)[None, :]
    m = i + L - j
    n_str = RES // DIL_PATTERNS[1][1]
    qa = L // n_str
    u_q = 4 * (i % qa) + i // qa
    m_g1 = u_q + L - (4 * (j % (2 * qa)) + j // (2 * qa))
    m_g1_first = np.where(j < L, u_q - (4 * (j % qa) + j // qa), -1)

    def bucket(dist):
        n = np.maximum(dist, 0)
        exact = REL_BUCKETS // 2
        log_ratio = (np.log(np.maximum(n, exact).astype(np.float32) / np.float32(exact))
                     / np.float32(math.log(REL_MAX_DIST / exact))).astype(np.float32)
        large = np.minimum(exact + (log_ratio * np.float32(REL_BUCKETS - exact)).astype(np.int32),
                           REL_BUCKETS - 1)
        return np.where(n < exact, n, large).astype(np.int32)

    def windowed(mm, gi):
        window, dil = DIL_PATTERNS[gi]
        return np.where((mm >= 0) & (mm <= window // dil), bucket(mm * dil), -1)

    pats = [np.where(m >= 0, bucket(m), -1), windowed(m, 0), windowed(m_g1, 1),
            windowed(m_g1_first, 1), windowed(m, 2)]
    return np.stack(pats).astype(np.int32)


_TILE_HEADS = (list(range(DIFF_HEADS + 2 * DIL_HEADS))
               + list(range(DIFF_HEADS + DIL_HEADS, DIFF_HEADS + 3 * DIL_HEADS)))
_TILE_PATS = [0] * DIFF_HEADS + [1] * DIL_HEADS + [2] * DIL_HEADS + [3] * DIL_HEADS + [4] * DIL_HEADS
_TILE_G0, _TILE_G1, _TILE_G1_FIRST, _TILE_G2 = (DIFF_HEADS + k * DIL_HEADS for k in range(4))


def _bias_kernel(head_ref, patid_ref, tbl_ref, pat_ref, tile_ref, far_ref):
    t = pl.program_id(0)
    h = head_ref[t]
    pat = pat_ref[0]
    mult = jnp.where(patid_ref[t] == 0, 1.0, 1.0 / _DIL_SCALE).astype(F32)
    tile = jnp.full(pat.shape, NEG_INF, F32)
    for b in range(REL_BUCKETS):
        tile = jnp.where(pat == b, tbl_ref[h, b] * mult, tile)
    tile_ref[0] = tile
    far_ref[0] = jnp.full(far_ref.shape[1:], tbl_ref[h, REL_BUCKETS - 1], F32)


def _bias_tiles(rel_bias):
    pats = jnp.asarray(_bucket_patterns())
    tbl = rel_bias.T
    n_tiles = len(_TILE_HEADS)
    shape = (1, DIL_BLOCK, 2 * DIL_BLOCK)
    return pl.pallas_call(
        _bias_kernel,
        grid_spec=pltpu.PrefetchScalarGridSpec(
            num_scalar_prefetch=2,
            grid=(n_tiles,),
            in_specs=[pl.BlockSpec(memory_space=pltpu.SMEM),
                      pl.BlockSpec(shape, lambda t, heads, patids: (patids[t], 0, 0))],
            out_specs=[pl.BlockSpec(shape, lambda t, heads, patids: (t, 0, 0)),
                       pl.BlockSpec((1, 8, LANE), lambda t, heads, patids: (t, 0, 0))]),
        out_shape=[jax.ShapeDtypeStruct((n_tiles,) + shape[1:], F32),
                   jax.ShapeDtypeStruct((n_tiles, 8, LANE), F32)],
        compiler_params=_cparams(("arbitrary",)),
        name="bias_tiles",
    )(jnp.asarray(_TILE_HEADS, jnp.int32), jnp.asarray(_TILE_PATS, jnp.int32), tbl, pats)


def _ret_kernel(*refs):
    for hh in range(_RET_HEADS_PER_STEP):
        _ret_head(hh, *refs)


def _ret_head(hh, q_ref, k_ref, v_ref, g_ref, cos_ref, sin_ref, dec_ref, kend_ref, qst_ref,
              cdec_ref, gain_ref, o_ref):
    c = RET_CHUNK
    n_chunks = q_ref.shape[1] // c
    ck = slice(hh * RET_DK, (hh + 1) * RET_DK)
    cv = slice(hh * RET_DV, (hh + 1) * RET_DV)
    scores, q_cross, kvs = [], [], []
    for n in range(n_chunks):
        rows = pl.ds(n * c, c)
        cos = cos_ref[rows, :]
        sin = sin_ref[rows, :]
        q = q_ref[0, rows, ck].astype(F32)
        k = k_ref[0, rows, ck].astype(F32)
        qr = q * cos + pltpu.roll(q, RET_DK // 2, 1) * sin
        kr = (k * cos + pltpu.roll(k, RET_DK // 2, 1) * sin) * (RET_DK ** -0.5)
        scores.append(_dot_nt(qr.astype(BF16), kr.astype(BF16)))
        q_cross.append((qr * qst_ref[hh]).astype(BF16))
        if n < n_chunks - 1:
            kvs.append(_dot_tn((kr * kend_ref[hh]).astype(BF16), v_ref[0, rows, cv]))

    states = [None]
    st = None
    for n in range(n_chunks - 1):
        st = kvs[n] if st is None else st * cdec_ref[hh] + kvs[n]
        states.append(st.astype(BF16))

    for n in range(n_chunks):
        rows = pl.ds(n * c, c)
        y = _dot((scores[n] * dec_ref[hh]).astype(BF16), v_ref[0, rows, cv])
        if states[n] is not None:
            y = y + _dot(q_cross[n], states[n])
        mu = jnp.mean(y, axis=-1, keepdims=True)
        yc = y - mu
        var = jnp.mean(yc * yc, axis=-1, keepdims=True)
        yn = yc * lax.rsqrt(var + GN_EPS) * gain_ref[:, cv]
        g = g_ref[0, rows, cv]
        o_ref[0, rows, cv] = ((g * jax.nn.sigmoid(g)).astype(F32) * yn).astype(o_ref.dtype)


def _ret_tables(S):
    H, dk, c = RET_HEADS, RET_DK, RET_CHUNK
    half = dk // 2
    pos = np.arange(S, dtype=np.float64)
    inv = ROPE_BASE ** (-np.arange(half, dtype=np.float64) / half)
    ang = pos[:, None] * inv[None, :]
    cos, sin = np.cos(ang), np.sin(ang)
    cos2 = np.concatenate([cos, cos], axis=-1)
    sin2 = np.concatenate([-sin, sin], axis=-1)
    log_gamma = np.log1p(-np.exp2(-5.0 - np.arange(H, dtype=np.float64)))
    i = np.arange(c, dtype=np.float64)
    rel = i[:, None] - i[None, :]
    decay = np.where(rel >= 0, np.exp(log_gamma[:, None, None] * np.maximum(rel, 0.0)), 0.0)
    k_to_end = np.exp((c - 1.0 - i)[None, :] * log_gamma[:, None])
    q_from_start = np.exp((i + 1.0)[None, :] * log_gamma[:, None])
    kend = np.broadcast_to(k_to_end[:, :, None], (H, c, dk))
    qst = np.broadcast_to(q_from_start[:, :, None], (H, c, dk))
    cdec = np.broadcast_to(np.exp(c * log_gamma)[:, None, None], (H, 1, RET_DV))
    return tuple(jnp.asarray(t, F32) for t in (cos2, sin2, decay, kend, qst, cdec))


def _retention(pm, tables, gain):
    B, S, _ = pm.shape
    H, dk, dv, c = RET_HEADS, RET_DK, RET_DV, RET_CHUNK
    cos2, sin2, decay, kend, qst, cdec = tables
    n = _RET_HEADS_PER_STEP
    qb = _M_RET // (n * dk)
    kb = qb + H // n
    vb = (_M_RET + 2 * H * dk) // (n * dv)
    gb = vb + H // n
    const2 = lambda b, h: (0, 0)
    head3 = lambda b, h: (h, 0, 0)
    return pl.pallas_call(
        _ret_kernel,
        grid=(B, H // n),
        in_specs=[pl.BlockSpec((1, S, n * dk), lambda b, h: (b, 0, qb + h)),
                  pl.BlockSpec((1, S, n * dk), lambda b, h: (b, 0, kb + h)),
                  pl.BlockSpec((1, S, n * dv), lambda b, h: (b, 0, vb + h)),
                  pl.BlockSpec((1, S, n * dv), lambda b, h: (b, 0, gb + h)),
                  pl.BlockSpec((S, dk), const2),
                  pl.BlockSpec((S, dk), const2),
                  pl.BlockSpec((n, c, c), head3),
                  pl.BlockSpec((n, c, dk), head3),
                  pl.BlockSpec((n, c, dk), head3),
                  pl.BlockSpec((n, 1, dv), head3),
                  pl.BlockSpec((1, n * dv), lambda b, h: (0, h))],
        out_specs=pl.BlockSpec((1, S, n * dv), lambda b, h: (b, 0, h)),
        out_shape=jax.ShapeDtypeStruct((B, S, H * dv), BF16),
        compiler_params=_cparams(("parallel", "parallel")),
        name="retention",
    )(pm, pm, pm, pm, cos2, sin2, decay, kend, qst, cdec, gain)


def _diff_kernel(q_ref, k_ref, v_ref, bias_ref, far_ref, lam_ref, gain_ref, o_ref, vv_ref, *,
                 lam_init):
    for hh in range(_DIFF_HEADS_PER_STEP):
        _diff_head(q_ref, k_ref, v_ref, bias_ref, far_ref, lam_ref, gain_ref, o_ref, vv_ref, hh,
                   lam_init)


def _diff_head(q_ref, k_ref, v_ref, bias_ref, far_ref, lam_ref, gain_ref, o_ref, vv_ref, hh,
               lam_init):
    S = q_ref.shape[1]
    QB = DIL_BLOCK
    dv = 2 * DIFF_DH
    cs = slice(hh * dv, (hh + 1) * dv)
    vv_ref[hh, :, :dv] = v_ref[0, :, cs]
    vv_ref[hh, :, dv:] = jnp.ones((S, dv), BF16)
    first = lax.broadcasted_iota(jnp.int32, (QB, 2 * DIFF_DH), 1) < DIFF_DH
    lp = lam_ref[...]
    lam = (jnp.exp(jnp.sum(lp[0:1] * lp[1:2], axis=-1, keepdims=True))
           - jnp.exp(jnp.sum(lp[2:3] * lp[3:4], axis=-1, keepdims=True)) + lam_init)
    bias = (bias_ref[hh] - far_ref[hh][0:1, 0:1]) * _LOG2E
    bias2 = jnp.concatenate([bias, bias], axis=0)
    gain = gain_ref[...] * (1.0 - lam_init)
    zero = jnp.zeros((), BF16)
    n_q = S // QB

    def bounds(qi):
        return max(qi - 1, 0) * QB, (qi + 1) * QB

    def scores(qi):
        qb = (q_ref[0, qi * QB:(qi + 1) * QB, cs].astype(F32)
              * (DIFF_DH ** -0.5 * _LOG2E)).astype(BF16)
        qq = jnp.concatenate([jnp.where(first, qb, zero), jnp.where(first, zero, qb)], axis=0)
        lo, hi = bounds(qi)
        sb = _dot_nt(qq, k_ref[0, lo:hi, cs]) + (bias2 if qi > 0 else bias2[:, QB:])
        sf = _dot_nt(qq, k_ref[0, 0:lo, cs]) if lo > 0 else None
        return sb, sf

    def finish(qi, sb, sf):
        lo, hi = bounds(qi)
        m = jnp.max(sb, axis=-1, keepdims=True)
        if sf is not None:
            m = jnp.maximum(m, jnp.max(sf, axis=-1, keepdims=True))
        acc = _dot(jnp.exp2((sb - m).astype(BF16)), vv_ref[hh, lo:hi, :])
        if sf is not None:
            acc = acc + _dot(jnp.exp2((sf - m).astype(BF16)), vv_ref[hh, 0:lo, :])
        r = acc[:, :dv] / acc[:, dv:]
        o = r[:QB] - lam * r[QB:]
        o = o * lax.rsqrt(jnp.mean(o * o, axis=-1, keepdims=True) + DIFF_EPS) * gain
        o_ref[0, qi * QB:(qi + 1) * QB, cs] = o.astype(o_ref.dtype)

    pending = {}
    for step in range(n_q + _DIFF_AHEAD):
        if step < n_q:
            pending[step] = scores(step)
        if step >= _DIFF_AHEAD:
            finish(step - _DIFF_AHEAD, *pending.pop(step - _DIFF_AHEAD))


def _diff_attention(pm, tiles, far, lam_params, gain, lam_init):
    B, S, _ = pm.shape
    H, hw = DIFF_HEADS, 2 * DIFF_DH
    n = _DIFF_HEADS_PER_STEP
    w = n * hw
    qb = _M_DIFF // w
    kb, vb = qb + H // n, qb + 2 * (H // n)
    return pl.pallas_call(
        functools.partial(_diff_kernel, lam_init=lam_init),
        grid=(B, H // n),
        in_specs=[pl.BlockSpec((1, S, w), lambda b, h: (b, 0, qb + h)),
                  pl.BlockSpec((1, S, w), lambda b, h: (b, 0, kb + h)),
                  pl.BlockSpec((1, S, w), lambda b, h: (b, 0, vb + h)),
                  pl.BlockSpec((n, DIL_BLOCK, 2 * DIL_BLOCK), lambda b, h: (h, 0, 0)),
                  pl.BlockSpec((n, 8, LANE), lambda b, h: (h, 0, 0)),
                  pl.BlockSpec((4, DIFF_DH), lambda b, h: (0, 0)),
                  pl.BlockSpec((1, hw), lambda b, h: (0, 0))],
        out_specs=pl.BlockSpec((1, S, w), lambda b, h: (b, 0, h)),
        out_shape=jax.ShapeDtypeStruct((B, S, H * hw), BF16),
        scratch_shapes=[pltpu.VMEM((n, S, 2 * hw), BF16)],
        compiler_params=_cparams(("parallel", "parallel")),
        name="diff_attention",
    )(pm, pm, pm, tiles, far, lam_params, gain)


def _run_pipelined(stages):
    pending = None
    for issue, consume in list(stages) + [(None, None)]:
        issued = issue() if issue is not None else None
        if pending is not None:
            pending[1](pending[0])
        pending = (issued, consume) if consume is not None else None


def _softmax_parts(s, bias):
    s = s + bias
    mx = jnp.max(s, axis=-1, keepdims=True)
    p = jnp.exp2((s - mx) * (_DIL_SCALE * _LOG2E))
    return p.astype(BF16), mx * _DIL_SCALE, jnp.sum(p, axis=-1, keepdims=True)


def _lse_combine(o_a, l_a, o_b, l_b):
    mx = jnp.maximum(l_a, l_b)
    w_a, w_b = jnp.exp(l_a - mx), jnp.exp(l_b - mx)
    den = w_a + w_b
    return (w_a * o_a + w_b * o_b) / den, mx + jnp.log(den)


def _dil_kernel(*refs):
    for hs in range(_DIL_SLOTS_PER_STEP):
        _dil_slot(hs, *refs)


def _dil_slot(hs, q0_ref, k0_ref, v0_ref, q1_ref, k1_ref, v1_ref, q2_ref, k2_ref, v2_ref,
              t0_ref, t1_ref, t1f_ref, t2_ref, unperm_ref, y_ref,
              o1_ref, l1_ref, x16_ref, nat_ref):
    L, dh = DIL_BLOCK, DIL_DH
    S = q0_ref.shape[1]
    n_a = S // RES
    dil1 = DIL_PATTERNS[1][1]
    n_str = RES // dil1
    qa = L // n_str
    cs = slice(hs * dh, (hs + 1) * dh)
    lane = lax.broadcasted_iota(jnp.int32, (L, dh), 1)
    zeros = jnp.zeros((L, dh), BF16)
    stages = []

    def g1_stage(r4):
        streams = [r4 + dil1 * s for s in range(n_str)]
        n_blocks = n_a // qa

        def gather(ref, lo, hi):
            return jnp.concatenate([ref[0, r, lo:hi, cs] for r in streams], axis=0)

        def band(n):
            return (0, qa) if n == 0 else ((n - 1) * qa, (n + 1) * qa)

        def issue():
            return [_dot_nt(gather(q1_ref, n * qa, (n + 1) * qa), gather(k1_ref, *band(n)))
                    for n in range(n_blocks)]

        def consume(ss):
            parts = [_softmax_parts(s, t1f_ref[hs][:, :L] if n == 0 else t1_ref[hs])
                     for n, s in enumerate(ss)]
            nums = [_dot(p, gather(v1_ref, *band(n))) for n, (p, _, _) in enumerate(parts)]
            for n, (num, (_, mx, den)) in enumerate(zip(nums, parts)):
                o = num / den
                lse = jnp.broadcast_to(mx + jnp.log(den), (L, dh))
                for i, r in enumerate(streams):
                    o1_ref[hs, r, n * qa:(n + 1) * qa, :] = o[i * qa:(i + 1) * qa]
                    l1_ref[hs, r, n * qa:(n + 1) * qa, :] = lse[i * qa:(i + 1) * qa]

        return issue, consume

    stages += [g1_stage(r4) for r4 in range(dil1)]

    def g2_stage(pairs):
        def blockdiag(ref, r):
            return jnp.concatenate([jnp.concatenate([ref[0, r, :, cs], zeros], axis=1),
                                    jnp.concatenate([zeros, ref[0, r + 1, :, cs]], axis=1)], axis=0)

        def issue():
            return [_dot_nt(jnp.concatenate([q2_ref[0, r, :, cs], q2_ref[0, r + 1, :, cs]], axis=1),
                            blockdiag(k2_ref, r)) for r in pairs]

        def consume(ss):
            bias = t2_ref[hs][:, L:]
            parts = [[_softmax_parts(s[:, i * L:(i + 1) * L], bias) for i in range(2)] for s in ss]
            outs = [_dot(jnp.concatenate([pp[0][0], pp[1][0]], axis=1), blockdiag(v2_ref, r))
                    for r, pp in zip(pairs, parts)]
            for r, pp, out in zip(pairs, parts, outs):
                for i in range(2):
                    _, mx, den = pp[i]
                    o2 = out[:, i * L:(i + 1) * L] / den
                    l2 = jnp.broadcast_to(mx + jnp.log(den), (L, dh))
                    o12, l12 = _lse_combine(o1_ref[hs, r + i], l1_ref[hs, r + i], o2, l2)
                    hi = l12.astype(BF16)
                    lo = (l12 - hi.astype(F32)).astype(BF16)
                    x16_ref[hs, r + i] = jnp.concatenate(
                        [o12.astype(BF16), jnp.where(lane < dh // 2, hi, lo)], axis=1)

        return issue, consume

    stages += [g2_stage(range(r0, r0 + RES // 2, 2)) for r0 in (0, RES // 2)]

    def unpermute(_):
        for s in range(S // PERM_ROWS):
            xs = jnp.concatenate([x16_ref[hs, r, s * RES:(s + 1) * RES, :] for r in range(RES)],
                                 axis=0)
            nat_ref[hs, s * PERM_ROWS:(s + 1) * PERM_ROWS, :] = _dot(unperm_ref[...], xs)

    stages.append((lambda: None, unpermute))

    def g0_stage(blocks):
        def band(n):
            return (0, L) if n == 0 else ((n - 1) * L, (n + 1) * L)

        def issue():
            return [_dot_nt(q0_ref[0, n * L:(n + 1) * L, cs], k0_ref[0, band(n)[0]:band(n)[1], cs])
                    for n in blocks]

        def consume(ss):
            parts = [_softmax_parts(s, t0_ref[hs][:, L:] if n == 0 else t0_ref[hs])
                     for n, s in zip(blocks, ss)]
            nums = [_dot(p, v0_ref[0, band(n)[0]:band(n)[1], cs])
                    for n, (p, _, _) in zip(blocks, parts)]
            for n, num, (_, mx, den) in zip(blocks, nums, parts):
                nat = nat_ref[hs, n * L:(n + 1) * L, :]
                l12 = nat[:, dh:dh + 1] + nat[:, dh + dh // 2:dh + dh // 2 + 1]
                y, _ = _lse_combine(num / den, mx + jnp.log(den), nat[:, :dh], l12)
                y_ref[0, n * L:(n + 1) * L, cs] = y.astype(y_ref.dtype)

        return issue, consume

    per_stage = 4
    stages += [g0_stage(range(n0, n0 + per_stage)) for n0 in range(0, S // L, per_stage)]
    _run_pipelined(stages)


def _dilated(pm, tiles, unperm):
    B, S, width = pm.shape
    L, dh, H = DIL_BLOCK, DIL_DH, DIL_HEADS
    n_a = S // RES
    c0 = _M_DIL0 // dh
    c1 = _MAIN_W // dh
    p16 = pm.reshape(B, RES, n_a, width)
    n = _DIL_SLOTS_PER_STEP
    w = n * dh
    nat = lambda c: pl.BlockSpec((1, S, w), lambda b, h: (b, 0, c // n + h))
    res = lambda c: pl.BlockSpec((1, RES, n_a, w), lambda b, h: (b, 0, 0, (c1 + c) // n + h))
    tile = lambda t: pl.BlockSpec((n, L, 2 * L), lambda b, h: (t // n + h, 0, 0))
    return pl.pallas_call(
        _dil_kernel,
        grid=(B, H // n),
        in_specs=[nat(c0), nat(c0 + H), nat(c0 + 2 * H),
                  res(0), res(H), res(2 * H), res(3 * H), res(4 * H), res(5 * H),
                  tile(_TILE_G0), tile(_TILE_G1), tile(_TILE_G1_FIRST), tile(_TILE_G2),
                  pl.BlockSpec((PERM_ROWS, PERM_ROWS), lambda b, h: (0, 0))],
        out_specs=pl.BlockSpec((1, S, w), lambda b, h: (b, 0, h)),
        out_shape=jax.ShapeDtypeStruct((B, S, H * dh), BF16),
        scratch_shapes=[pltpu.VMEM((n, RES, n_a, dh), F32), pltpu.VMEM((n, RES, n_a, dh), F32),
                        pltpu.VMEM((n, RES, n_a, 2 * dh), BF16), pltpu.VMEM((n, S, 2 * dh), F32)],
        compiler_params=_cparams(("parallel", "parallel")),
        name="dilated",
    )(pm, pm, pm, p16, p16, p16, p16, p16, p16, tiles, tiles, tiles, tiles, unperm)


def _merge_kernel(gate_ref, yr_ref, yd_ref, yl_ref, x_ref, wr_ref, wd_ref, wl_ref, wo_ref, out_ref):
    D = D_MODEL
    gates = jax.nn.sigmoid(gate_ref[...].astype(F32))
    merged = (gates[:, 0:D] * _dot(yr_ref[...], wr_ref[...])
              + gates[:, D:2 * D] * _dot(yd_ref[...], wd_ref[...])
              + gates[:, 2 * D:3 * D] * _dot(yl_ref[...], wl_ref[...]))
    out_ref[...] = x_ref[...] + _dot(merged.astype(BF16), wo_ref[...])


def _layer_spec(a, l):
    return pl.BlockSpec((None,) + a.shape[1:], lambda i: (l, 0, 0), pipeline_mode=pl.Buffered(1))


def _merge(pm2, y_ret, y_diff, y_dil, x2, wr, wd, wl, wo, l, tm=512):
    T, D = x2.shape
    row = lambda w: pl.BlockSpec((tm, w), lambda i: (i, 0))
    return pl.pallas_call(
        _merge_kernel,
        grid=(T // tm,),
        in_specs=[row(_GATE_W), row(D), row(D), row(y_dil.shape[1]), row(D),
                  _layer_spec(wr, l), _layer_spec(wd, l), _layer_spec(wl, l), _layer_spec(wo, l)],
        out_specs=row(D),
        out_shape=jax.ShapeDtypeStruct((T, D), F32),
        compiler_params=_cparams(("parallel",)),
        name="merge_out",
    )(pm2, y_ret, y_diff, y_dil, x2, wr, wd, wl, wo)


def _ffn_kernel(x_ref, g_ref, wg_ref, wu_ref, wd_ref, gf_ref, o_ref, *, final):
    x = x_ref[...]
    h = _rms(x, g_ref[...], RMS_EPS).astype(BF16)
    a = _dot(h, wg_ref[...])
    u = _dot(h, wu_ref[...])
    z = (a * jax.nn.sigmoid(a) * u).astype(BF16)
    y = x + _dot(z, wd_ref[...])
    if final:
        y = _rms(y, gf_ref[...], RMS_EPS)
    o_ref[...] = y


def _ffn(x2, g, wg, wu, wd, g_final, l, final, tm=1024):
    T, D = x2.shape
    row = pl.BlockSpec((tm, D), lambda i: (i, 0))
    return pl.pallas_call(
        functools.partial(_ffn_kernel, final=final),
        grid=(T // tm,),
        in_specs=[row, _layer_spec(g, l), _layer_spec(wg, l), _layer_spec(wu, l),
                  _layer_spec(wd, l), _layer_spec(g_final, 0)],
        out_specs=row,
        out_shape=jax.ShapeDtypeStruct((T, D), F32),
        compiler_params=_cparams(("parallel",)),
        name="ffn_final" if final else "ffn",
    )(x2, g, wg, wu, wd, g_final)


def kernel(x, w_in, w_branch_ret, w_branch_diff, w_branch_dil, w_out, norm_mix, norm_ffn, ret_gn_gain, diff_lambda, diff_subln_gain, rel_bias, w_ffn_gate, w_ffn_up, w_ffn_down, norm_final):
    B, S, D = x.shape
    T = B * S
    tiles, far = _bias_tiles(rel_bias)
    ret_tables = _ret_tables(S)
    perm = _residue_perm()
    perm_fwd, perm_back = jnp.asarray(perm, BF16), jnp.asarray(perm.T, BF16)
    x2 = x.reshape(T, D)
    w_branch_ret, w_branch_diff, w_branch_dil, w_out, w_ffn_gate, w_ffn_up, w_ffn_down = (
        w.astype(BF16) for w in (w_branch_ret, w_branch_diff, w_branch_dil, w_out,
                                 w_ffn_gate, w_ffn_up, w_ffn_down))
    norm_mix, norm_ffn = norm_mix.reshape(DEPTH, 1, D), norm_ffn.reshape(DEPTH, 1, D)
    norm_final = norm_final.reshape(1, 1, D)
    for l in range(DEPTH):
        pm2 = _inproj(x2, norm_mix, perm_fwd, w_in, l, B)
        pm = pm2.reshape(B, S, -1)

        y_ret = _retention(pm, ret_tables, ret_gn_gain[l].reshape(1, -1))
        lam_init = 0.8 - 0.6 * math.exp(-0.3 * l)
        y_diff = _diff_attention(pm, tiles, far, diff_lambda[l], diff_subln_gain[l].reshape(1, -1),
                                 lam_init)
        y_dil = _dilated(pm, tiles, perm_back)

        x2 = _merge(pm2, y_ret.reshape(T, -1), y_diff.reshape(T, -1), y_dil.reshape(T, -1), x2,
                    w_branch_ret, w_branch_diff, w_branch_dil, w_out, l)
        x2 = _ffn(x2, norm_ffn, w_ffn_gate, w_ffn_up, w_ffn_down, norm_final, l,
                  final=(l == DEPTH - 1))
    return x2.reshape(B, S, D)
```

```python
import functools
import math

import numpy as np
import jax
import jax.numpy as jnp
from jax import lax
from jax.experimental import pallas as pl
from jax.experimental.pallas import tpu as pltpu

F32 = jnp.float32
BF16 = jnp.bfloat16

D_MODEL = 1024
DEPTH = 2
RET_HEADS, RET_DK, RET_DV, RET_CHUNK = 4, 128, 256, 128
ROPE_BASE = 10000.0
GN_EPS = 1e-5
DIFF_HEADS, DIFF_DH = 8, 64
DIFF_EPS = 1e-5
DIL_PATTERNS = ((128, 1), (512, 4), (2048, 16))
N_DIL = len(DIL_PATTERNS)
DIL_HEADS, DIL_DH, DIL_BLOCK = 4, 128, 128
REL_BUCKETS, REL_MAX_DIST = 32, 128
N_BIAS_HEADS = DIFF_HEADS + N_DIL * DIL_HEADS
FFN_HIDDEN = -(-8 * D_MODEL // (3 * 256)) * 256
RMS_EPS = 1e-6
NEG_INF = -1e30

_RET_W = 2 * RET_HEADS * RET_DK + 2 * RET_HEADS * RET_DV
_DIFF_W = 3 * DIFF_HEADS * 2 * DIFF_DH
_DIL_W = 3 * DIL_HEADS * DIL_DH
_GATE_W = 3 * D_MODEL
_OFF_DIFF = _RET_W
_OFF_DIL = _RET_W + _DIFF_W
_OFF_GATE = _OFF_DIL + N_DIL * _DIL_W
_MAIN_W = _GATE_W + _RET_W + _DIFF_W + _DIL_W
_M_RET = _GATE_W
_M_DIFF = _GATE_W + _RET_W
_M_DIL0 = _GATE_W + _RET_W + _DIFF_W

_LOG2E = math.log2(math.e)
_DIL_SCALE = DIL_DH ** -0.5
_DIFF_AHEAD = 3
_DIFF_HEADS_PER_STEP = 4
_DIL_SLOTS_PER_STEP = 4
_RET_HEADS_PER_STEP = 2

LANE = 128
VMEM_LIMIT = 56 * 1024 * 1024


def _cparams(sem):
    return pltpu.CompilerParams(dimension_semantics=sem, vmem_limit_bytes=VMEM_LIMIT)


def _rms(x, g, eps):
    return x * lax.rsqrt(jnp.mean(x * x, axis=-1, keepdims=True) + eps) * g


def _dot(a, b):
    return jnp.dot(a, b, preferred_element_type=F32)


def _dot_nt(a, b):
    return lax.dot_general(a, b, (((1,), (1,)), ((), ())), preferred_element_type=F32)


def _dot_tn(a, b):
    return lax.dot_general(a, b, (((0,), (0,)), ((), ())), preferred_element_type=F32)


RES = 16
PERM_ROWS = RES * RES


def _residue_perm():
    p = np.zeros((PERM_ROWS, PERM_ROWS), np.float32)
    a, r = np.meshgrid(np.arange(RES), np.arange(RES), indexing="ij")
    p[(r * RES + a).ravel(), (a * RES + r).ravel()] = 1.0
    return p


def _inproj_kernel(x_ref, g_ref, perm_ref, w_ref, o_ref, h_ref, hp_ref, *, n_main, row_chunk):
    j = pl.program_id(1)
    S = x_ref.shape[0]
    per_res = S // RES

    @pl.when(j == 0)
    def _():
        for s in range(S // PERM_ROWS):
            rows = pl.ds(s * PERM_ROWS, PERM_ROWS)
            h = _rms(x_ref[rows, :], g_ref[...], RMS_EPS).astype(BF16)
            h_ref[rows, :] = h
            hs = _dot(perm_ref[...], h).astype(BF16)
            for r in range(RES):
                hp_ref[pl.ds(r * per_res + s * RES, RES), :] = hs[r * RES:(r + 1) * RES]

    def project(src_ref):
        w = w_ref[...].astype(BF16)
        for c in range(S // row_chunk):
            rows = pl.ds(c * row_chunk, row_chunk)
            o_ref[rows, :] = _dot(src_ref[rows, :], w).astype(o_ref.dtype)

    @pl.when(j < n_main)
    def _():
        project(h_ref)

    @pl.when(j >= n_main)
    def _():
        project(hp_ref)


def _inproj(x2, g, perm, w, l, B, tn=1536, row_chunk=1024):
    T, D = x2.shape
    S = T // B
    n_main = _MAIN_W // tn
    n_tiles = w.shape[2] // tn
    first = _OFF_GATE // tn
    return pl.pallas_call(
        functools.partial(_inproj_kernel, n_main=n_main, row_chunk=row_chunk),
        grid=(B, n_tiles),
        in_specs=[pl.BlockSpec((S, D), lambda i, j: (i, 0)),
                  pl.BlockSpec((None, 1, D), lambda i, j: (l, 0, 0)),
                  pl.BlockSpec((PERM_ROWS, PERM_ROWS), lambda i, j: (0, 0)),
                  pl.BlockSpec((None, D, tn), lambda i, j: (l, 0, (j + first) % n_tiles))],
        out_specs=pl.BlockSpec((S, tn), lambda i, j: (i, j)),
        out_shape=jax.ShapeDtypeStruct((T, w.shape[2]), BF16),
        scratch_shapes=[pltpu.VMEM((S, D), BF16), pltpu.VMEM((S, D), BF16)],
        compiler_params=_cparams(("parallel", "arbitrary")),
        name="inproj",
    )(x2, g, perm, w)


def _bucket_patterns():
    L = DIL_BLOCK
    i = np.arange(L)[:, None]
    j = np.arange(2 * L)[None, :]
    m = i + L - j
    n_str = RES // DIL_PATTERNS[1][1]
    qa = L // n_str
    u_q = 4 * (i % qa) + i // qa
    m_g1 = u_q + L - (4 * (j % (2 * qa)) + j // (2 * qa))
    m_g1_first = np.where(j < L, u_q - (4 * (j % qa) + j // qa), -1)

    def bucket(dist):
        n = np.maximum(dist, 0)
        exact = REL_BUCKETS // 2
        log_ratio = (np.log(np.maximum(n, exact).astype(np.float32) / np.float32(exact))
                     / np.float32(math.log(REL_MAX_DIST / exact))).astype(np.float32)
        large = np.minimum(exact + (log_ratio * np.float32(REL_BUCKETS - exact)).astype(np.int32),
                           REL_BUCKETS - 1)
        return np.where(n < exact, n, large).astype(np.int32)

    def windowed(mm, gi):
        window, dil = DIL_PATTERNS[gi]
        return np.where((mm >= 0) & (mm <= window // dil), bucket(mm * dil), -1)

    pats = [np.where(m >= 0, bucket(m), -1), windowed(m, 0), windowed(m_g1, 1),
            windowed(m_g1_first, 1), windowed(m, 2)]
    return np.stack(pats).astype(np.int32)


_TILE_HEADS = (list(range(DIFF_HEADS + 2 * DIL_HEADS))
               + list(range(DIFF_HEADS + DIL_HEADS, DIFF_HEADS + 3 * DIL_HEADS)))
_TILE_PATS = [0] * DIFF_HEADS + [1] * DIL_HEADS + [2] * DIL_HEADS + [3] * DIL_HEADS + [4] * DIL_HEADS
_TILE_G0, _TILE_G1, _TILE_G1_FIRST, _TILE_G2 = (DIFF_HEADS + k * DIL_HEADS for k in range(4))


def _bias_kernel(head_ref, patid_ref, tbl_ref, pat_ref, tile_ref, far_ref):
    t = pl.program_id(0)
    h = head_ref[t]
    pat = pat_ref[0]
    mult = jnp.where(patid_ref[t] == 0, 1.0, 1.0 / _DIL_SCALE).astype(F32)
    tile = jnp.full(pat.shape, NEG_INF, F32)
    for b in range(REL_BUCKETS):
        tile = jnp.where(pat == b, tbl_ref[h, b] * mult, tile)
    tile_ref[0] = tile
    far_ref[0] = jnp.full(far_ref.shape[1:], tbl_ref[h, REL_BUCKETS - 1], F32)


def _bias_tiles(rel_bias):
    pats = jnp.asarray(_bucket_patterns())
    tbl = rel_bias.T
    n_tiles = len(_TILE_HEADS)
    shape = (1, DIL_BLOCK, 2 * DIL_BLOCK)
    return pl.pallas_call(
        _bias_kernel,
        grid_spec=pltpu.PrefetchScalarGridSpec(
            num_scalar_prefetch=2,
            grid=(n_tiles,),
            in_specs=[pl.BlockSpec(memory_space=pltpu.SMEM),
                      pl.BlockSpec(shape, lambda t, heads, patids: (patids[t], 0, 0))],
            out_specs=[pl.BlockSpec(shape, lambda t, heads, patids: (t, 0, 0)),
                       pl.BlockSpec((1, 8, LANE), lambda t, heads, patids: (t, 0, 0))]),
        out_shape=[jax.ShapeDtypeStruct((n_tiles,) + shape[1:], F32),
                   jax.ShapeDtypeStruct((n_tiles, 8, LANE), F32)],
        compiler_params=_cparams(("arbitrary",)),
        name="bias_tiles",
    )(jnp.asarray(_TILE_HEADS, jnp.int32), jnp.asarray(_TILE_PATS, jnp.int32), tbl, pats)


def _ret_kernel(*refs):
    for hh in range(_RET_HEADS_PER_STEP):
        _ret_head(hh, *refs)


def _ret_head(hh, q_ref, k_ref, v_ref, g_ref, cos_ref, sin_ref, dec_ref, kend_ref, qst_ref,
              cdec_ref, gain_ref, o_ref):
    c = RET_CHUNK
    n_chunks = q_ref.shape[1] // c
    ck = slice(hh * RET_DK, (hh + 1) * RET_DK)
    cv = slice(hh * RET_DV, (hh + 1) * RET_DV)
    scores, q_cross, kvs = [], [], []
    for n in range(n_chunks):
        rows = pl.ds(n * c, c)
        cos = cos_ref[rows, :]
        sin = sin_ref[rows, :]
        q = q_ref[0, rows, ck].astype(F32)
        k = k_ref[0, rows, ck].astype(F32)
        qr = q * cos + pltpu.roll(q, RET_DK // 2, 1) * sin
        kr = (k * cos + pltpu.roll(k, RET_DK // 2, 1) * sin) * (RET_DK ** -0.5)
        scores.append(_dot_nt(qr.astype(BF16), kr.astype(BF16)))
        q_cross.append((qr * qst_ref[hh]).astype(BF16))
        if n < n_chunks - 1:
            kvs.append(_dot_tn((kr * kend_ref[hh]).astype(BF16), v_ref[0, rows, cv]))

    states = [None]
    st = None
    for n in range(n_chunks - 1):
        st = kvs[n] if st is None else st * cdec_ref[hh] + kvs[n]
        states.append(st.astype(BF16))

    for n in range(n_chunks):
        rows = pl.ds(n * c, c)
        y = _dot((scores[n] * dec_ref[hh]).astype(BF16), v_ref[0, rows, cv])
        if states[n] is not None:
            y = y + _dot(q_cross[n], states[n])
        mu = jnp.mean(y, axis=-1, keepdims=True)
        yc = y - mu
        var = jnp.mean(yc * yc, axis=-1, keepdims=True)
        yn = yc * lax.rsqrt(var + GN_EPS) * gain_ref[:, cv]
        g = g_ref[0, rows, cv]
        o_ref[0, rows, cv] = ((g * jax.nn.sigmoid(g)).astype(F32) * yn).astype(o_ref.dtype)


def _ret_tables(S):
    H, dk, c = RET_HEADS, RET_DK, RET_CHUNK
    half = dk // 2
    pos = np.arange(S, dtype=np.float64)
    inv = ROPE_BASE ** (-np.arange(half, dtype=np.float64) / half)
    ang = pos[:, None] * inv[None, :]
    cos, sin = np.cos(ang), np.sin(ang)
    cos2 = np.concatenate([cos, cos], axis=-1)
    sin2 = np.concatenate([-sin, sin], axis=-1)
    log_gamma = np.log1p(-np.exp2(-5.0 - np.arange(H, dtype=np.float64)))
    i = np.arange(c, dtype=np.float64)
    rel = i[:, None] - i[None, :]
    decay = np.where(rel >= 0, np.exp(log_gamma[:, None, None] * np.maximum(rel, 0.0)), 0.0)
    k_to_end = np.exp((c - 1.0 - i)[None, :] * log_gamma[:, None])
    q_from_start = np.exp((i + 1.0)[None, :] * log_gamma[:, None])
    kend = np.broadcast_to(k_to_end[:, :, None], (H, c, dk))
    qst = np.broadcast_to(q_from_start[:, :, None], (H, c, dk))
    cdec = np.broadcast_to(np.exp(c * log_gamma)[:, None, None], (H, 1, RET_DV))
    return tuple(jnp.asarray(t, F32) for t in (cos2, sin2, decay, kend, qst, cdec))


def _retention(pm, tables, gain):
    B, S, _ = pm.shape
    H, dk, dv, c = RET_HEADS, RET_DK, RET_DV, RET_CHUNK
    cos2, sin2, decay, kend, qst, cdec = tables
    n = _RET_HEADS_PER_STEP
    qb = _M_RET // (n * dk)
    kb = qb + H // n
    vb = (_M_RET + 2 * H * dk) // (n * dv)
    gb = vb + H // n
    const2 = lambda b, h: (0, 0)
    head3 = lambda b, h: (h, 0, 0)
    return pl.pallas_call(
        _ret_kernel,
        grid=(B, H // n),
        in_specs=[pl.BlockSpec((1, S, n * dk), lambda b, h: (b, 0, qb + h)),
                  pl.BlockSpec((1, S, n * dk), lambda b, h: (b, 0, kb + h)),
                  pl.BlockSpec((1, S, n * dv), lambda b, h: (b, 0, vb + h)),
                  pl.BlockSpec((1, S, n * dv), lambda b, h: (b, 0, gb + h)),
                  pl.BlockSpec((S, dk), const2),
                  pl.BlockSpec((S, dk), const2),
                  pl.BlockSpec((n, c, c), head3),
                  pl.BlockSpec((n, c, dk), head3),
                  pl.BlockSpec((n, c, dk), head3),
                  pl.BlockSpec((n, 1, dv), head3),
                  pl.BlockSpec((1, n * dv), lambda b, h: (0, h))],
        out_specs=pl.BlockSpec((1, S, n * dv), lambda b, h: (b, 0, h)),
        out_shape=jax.ShapeDtypeStruct((B, S, H * dv), BF16),
        compiler_params=_cparams(("parallel", "parallel")),
        name="retention",
    )(pm, pm, pm, pm, cos2, sin2, decay, kend, qst, cdec, gain)


def _diff_kernel(q_ref, k_ref, v_ref, bias_ref, far_ref, lam_ref, gain_ref, o_ref, vv_ref, *,
                 lam_init):
    for hh in range(_DIFF_HEADS_PER_STEP):
        _diff_head(q_ref, k_ref, v_ref, bias_ref, far_ref, lam_ref, gain_ref, o_ref, vv_ref, hh,
                   lam_init)


def _diff_head(q_ref, k_ref, v_ref, bias_ref, far_ref, lam_ref, gain_ref, o_ref, vv_ref, hh,
               lam_init):
    S = q_ref.shape[1]
    QB = DIL_BLOCK
    dv = 2 * DIFF_DH
    cs = slice(hh * dv, (hh + 1) * dv)
    vv_ref[hh, :, :dv] = v_ref[0, :, cs]
    vv_ref[hh, :, dv:] = jnp.ones((S, dv), BF16)
    first = lax.broadcasted_iota(jnp.int32, (QB, 2 * DIFF_DH), 1) < DIFF_DH
    lp = lam_ref[...]
    lam = (jnp.exp(jnp.sum(lp[0:1] * lp[1:2], axis=-1, keepdims=True))
           - jnp.exp(jnp.sum(lp[2:3] * lp[3:4], axis=-1, keepdims=True)) + lam_init)
    bias = (bias_ref[hh] - far_ref[hh][0:1, 0:1]) * _LOG2E
    bias2 = jnp.concatenate([bias, bias], axis=0)
    gain = gain_ref[...] * (1.0 - lam_init)
    zero = jnp.zeros((), BF16)
    n_q = S // QB

    def bounds(qi):
        return max(qi - 1, 0) * QB, (qi + 1) * QB

    def scores(qi):
        qb = (q_ref[0, qi * QB:(qi + 1) * QB, cs].astype(F32)
              * (DIFF_DH ** -0.5 * _LOG2E)).astype(BF16)
        qq = jnp.concatenate([jnp.where(first, qb, zero), jnp.where(first, zero, qb)], axis=0)
        lo, hi = bounds(qi)
        sb = _dot_nt(qq, k_ref[0, lo:hi, cs]) + (bias2 if qi > 0 else bias2[:, QB:])
        sf = _dot_nt(qq, k_ref[0, 0:lo, cs]) if lo > 0 else None
        return sb, sf

    def finish(qi, sb, sf):
        lo, hi = bounds(qi)
        m = jnp.max(sb, axis=-1, keepdims=True)
        if sf is not None:
            m = jnp.maximum(m, jnp.max(sf, axis=-1, keepdims=True))
        acc = _dot(jnp.exp2((sb - m).astype(BF16)), vv_ref[hh, lo:hi, :])
        if sf is not None:
            acc = acc + _dot(jnp.exp2((sf - m).astype(BF16)), vv_ref[hh, 0:lo, :])
        r = acc[:, :dv] / acc[:, dv:]
        o = r[:QB] - lam * r[QB:]
        o = o * lax.rsqrt(jnp.mean(o * o, axis=-1, keepdims=True) + DIFF_EPS) * gain
        o_ref[0, qi * QB:(qi + 1) * QB, cs] = o.astype(o_ref.dtype)

    pending = {}
    for step in range(n_q + _DIFF_AHEAD):
        if step < n_q:
            pending[step] = scores(step)
        if step >= _DIFF_AHEAD:
            finish(step - _DIFF_AHEAD, *pending.pop(step - _DIFF_AHEAD))


def _diff_attention(pm, tiles, far, lam_params, gain, lam_init):
    B, S, _ = pm.shape
    H, hw = DIFF_HEADS, 2 * DIFF_DH
    n = _DIFF_HEADS_PER_STEP
    w = n * hw
    qb = _M_DIFF // w
    kb, vb = qb + H // n, qb + 2 * (H // n)
    return pl.pallas_call(
        functools.partial(_diff_kernel, lam_init=lam_init),
        grid=(B, H // n),
        in_specs=[pl.BlockSpec((1, S, w), lambda b, h: (b, 0, qb + h)),
                  pl.BlockSpec((1, S, w), lambda b, h: (b, 0, kb + h)),
                  pl.BlockSpec((1, S, w), lambda b, h: (b, 0, vb + h)),
                  pl.BlockSpec((n, DIL_BLOCK, 2 * DIL_BLOCK), lambda b, h: (h, 0, 0)),
                  pl.BlockSpec((n, 8, LANE), lambda b, h: (h, 0, 0)),
                  pl.BlockSpec((4, DIFF_DH), lambda b, h: (0, 0)),
                  pl.BlockSpec((1, hw), lambda b, h: (0, 0))],
        out_specs=pl.BlockSpec((1, S, w), lambda b, h: (b, 0, h)),
        out_shape=jax.ShapeDtypeStruct((B, S, H * hw), BF16),
        scratch_shapes=[pltpu.VMEM((n, S, 2 * hw), BF16)],
        compiler_params=_cparams(("parallel", "parallel")),
        name="diff_attention",
    )(pm, pm, pm, tiles, far, lam_params, gain)


def _run_pipelined(stages):
    pending = None
    for issue, consume in list(stages) + [(None, None)]:
        issued = issue() if issue is not None else None
        if pending is not None:
            pending[1](pending[0])
        pending = (issued, consume) if consume is not None else None


def _softmax_parts(s, bias):
    s = s + bias
    mx = jnp.max(s, axis=-1, keepdims=True)
    p = jnp.exp2((s - mx) * (_DIL_SCALE * _LOG2E))
    return p.astype(BF16), mx * _DIL_SCALE, jnp.sum(p, axis=-1, keepdims=True)


def _lse_combine(o_a, l_a, o_b, l_b):
    mx = jnp.maximum(l_a, l_b)
    w_a, w_b = jnp.exp(l_a - mx), jnp.exp(l_b - mx)
    den = w_a + w_b
    return (w_a * o_a + w_b * o_b) / den, mx + jnp.log(den)


def _dil_kernel(*refs):
    for hs in range(_DIL_SLOTS_PER_STEP):
        _dil_slot(hs, *refs)


def _dil_slot(hs, q0_ref, k0_ref, v0_ref, q1_ref, k1_ref, v1_ref, q2_ref, k2_ref, v2_ref,
              t0_ref, t1_ref, t1f_ref, t2_ref, unperm_ref, y_ref,
              o1_ref, l1_ref, x16_ref, nat_ref):
    L, dh = DIL_BLOCK, DIL_DH
    S = q0_ref.shape[1]
    n_a = S // RES
    dil1 = DIL_PATTERNS[1][1]
    n_str = RES // dil1
    qa = L // n_str
    cs = slice(hs * dh, (hs + 1) * dh)
    lane = lax.broadcasted_iota(jnp.int32, (L, dh), 1)
    zeros = jnp.zeros((L, dh), BF16)
    stages = []

    def g1_stage(r4):
        streams = [r4 + dil1 * s for s in range(n_str)]
        n_blocks = n_a // qa

        def gather(ref, lo, hi):
            return jnp.concatenate([ref[0, r, lo:hi, cs] for r in streams], axis=0)

        def band(n):
            return (0, qa) if n == 0 else ((n - 1) * qa, (n + 1) * qa)

        def issue():
            return [_dot_nt(gather(q1_ref, n * qa, (n + 1) * qa), gather(k1_ref, *band(n)))
                    for n in range(n_blocks)]

        def consume(ss):
            parts = [_softmax_parts(s, t1f_ref[hs][:, :L] if n == 0 else t1_ref[hs])
                     for n, s in enumerate(ss)]
            nums = [_dot(p, gather(v1_ref, *band(n))) for n, (p, _, _) in enumerate(parts)]
            for n, (num, (_, mx, den)) in enumerate(zip(nums, parts)):
                o = num / den
                lse = jnp.broadcast_to(mx + jnp.log(den), (L, dh))
                for i, r in enumerate(streams):
                    o1_ref[hs, r, n * qa:(n + 1) * qa, :] = o[i * qa:(i + 1) * qa]
                    l1_ref[hs, r, n * qa:(n + 1) * qa, :] = lse[i * qa:(i + 1) * qa]

        return issue, consume

    stages += [g1_stage(r4) for r4 in range(dil1)]

    def g2_stage(pairs):
        def blockdiag(ref, r):
            return jnp.concatenate([jnp.concatenate([ref[0, r, :, cs], zeros], axis=1),
                                    jnp.concatenate([zeros, ref[0, r + 1, :, cs]], axis=1)], axis=0)

        def issue():
            return [_dot_nt(jnp.concatenate([q2_ref[0, r, :, cs], q2_ref[0, r + 1, :, cs]], axis=1),
                            blockdiag(k2_ref, r)) for r in pairs]

        def consume(ss):
            bias = t2_ref[hs][:, L:]
            parts = [[_softmax_parts(s[:, i * L:(i + 1) * L], bias) for i in range(2)] for s in ss]
            outs = [_dot(jnp.concatenate([pp[0][0], pp[1][0]], axis=1), blockdiag(v2_ref, r))
                    for r, pp in zip(pairs, parts)]
            for r, pp, out in zip(pairs, parts, outs):
                for i in range(2):
                    _, mx, den = pp[i]
                    o2 = out[:, i * L:(i + 1) * L] / den
                    l2 = jnp.broadcast_to(mx + jnp.log(den), (L, dh))
                    o12, l12 = _lse_combine(o1_ref[hs, r + i], l1_ref[hs, r + i], o2, l2)
                    hi = l12.astype(BF16)
                    lo = (l12 - hi.astype(F32)).astype(BF16)
                    x16_ref[hs, r + i] = jnp.concatenate(
                        [o12.astype(BF16), jnp.where(lane < dh // 2, hi, lo)], axis=1)

        return issue, consume

    stages += [g2_stage(range(r0, r0 + RES // 2, 2)) for r0 in (0, RES // 2)]

    def unpermute(_):
        for s in range(S // PERM_ROWS):
            xs = jnp.concatenate([x16_ref[hs, r, s * RES:(s + 1) * RES, :] for r in range(RES)],
                                 axis=0)
            nat_ref[hs, s * PERM_ROWS:(s + 1) * PERM_ROWS, :] = _dot(unperm_ref[...], xs)

    stages.append((lambda: None, unpermute))

    def g0_stage(blocks):
        def band(n):
            return (0, L) if n == 0 else ((n - 1) * L, (n + 1) * L)

        def issue():
            return [_dot_nt(q0_ref[0, n * L:(n + 1) * L, cs], k0_ref[0, band(n)[0]:band(n)[1], cs])
                    for n in blocks]

        def consume(ss):
            parts = [_softmax_parts(s, t0_ref[hs][:, L:] if n == 0 else t0_ref[hs])
                     for n, s in zip(blocks, ss)]
            nums = [_dot(p, v0_ref[0, band(n)[0]:band(n)[1], cs])
                    for n, (p, _, _) in zip(blocks, parts)]
            for n, num, (_, mx, den) in zip(blocks, nums, parts):
                nat = nat_ref[hs, n * L:(n + 1) * L, :]
                l12 = nat[:, dh:dh + 1] + nat[:, dh + dh // 2:dh + dh // 2 + 1]
                y, _ = _lse_combine(num / den, mx + jnp.log(den), nat[:, :dh], l12)
                y_ref[0, n * L:(n + 1) * L, cs] = y.astype(y_ref.dtype)

        return issue, consume

    per_stage = 4
    stages += [g0_stage(range(n0, n0 + per_stage)) for n0 in range(0, S // L, per_stage)]
    _run_pipelined(stages)


def _dilated(pm, tiles, unperm):
    B, S, width = pm.shape
    L, dh, H = DIL_BLOCK, DIL_DH, DIL_HEADS
    n_a = S // RES
    c0 = _M_DIL0 // dh
    c1 = _MAIN_W // dh
    p16 = pm.reshape(B, RES, n_a, width)
    n = _DIL_SLOTS_PER_STEP
    w = n * dh
    nat = lambda c: pl.BlockSpec((1, S, w), lambda b, h: (b, 0, c // n + h))
    res = lambda c: pl.BlockSpec((1, RES, n_a, w), lambda b, h: (b, 0, 0, (c1 + c) // n + h))
    tile = lambda t: pl.BlockSpec((n, L, 2 * L), lambda b, h: (t // n + h, 0, 0))
    return pl.pallas_call(
        _dil_kernel,
        grid=(B, H // n),
        in_specs=[nat(c0), nat(c0 + H), nat(c0 + 2 * H),
                  res(0), res(H), res(2 * H), res(3 * H), res(4 * H), res(5 * H),
                  tile(_TILE_G0), tile(_TILE_G1), tile(_TILE_G1_FIRST), tile(_TILE_G2),
                  pl.BlockSpec((PERM_ROWS, PERM_ROWS), lambda b, h: (0, 0))],
        out_specs=pl.BlockSpec((1, S, w), lambda b, h: (b, 0, h)),
        out_shape=jax.ShapeDtypeStruct((B, S, H * dh), BF16),
        scratch_shapes=[pltpu.VMEM((n, RES, n_a, dh), F32), pltpu.VMEM((n, RES, n_a, dh), F32),
                        pltpu.VMEM((n, RES, n_a, 2 * dh), BF16), pltpu.VMEM((n, S, 2 * dh), F32)],
        compiler_params=_cparams(("parallel", "parallel")),
        name="dilated",
    )(pm, pm, pm, p16, p16, p16, p16, p16, p16, tiles, tiles, tiles, tiles, unperm)


def _merge_kernel(gate_ref, yr_ref, yd_ref, yl_ref, x_ref, wr_ref, wd_ref, wl_ref, wo_ref, out_ref):
    D = D_MODEL
    gates = jax.nn.sigmoid(gate_ref[...].astype(F32))
    merged = (gates[:, 0:D] * _dot(yr_ref[...], wr_ref[...])
              + gates[:, D:2 * D] * _dot(yd_ref[...], wd_ref[...])
              + gates[:, 2 * D:3 * D] * _dot(yl_ref[...], wl_ref[...]))
    out_ref[...] = x_ref[...] + _dot(merged.astype(BF16), wo_ref[...])


def _layer_spec(a, l):
    return pl.BlockSpec((None,) + a.shape[1:], lambda i: (l, 0, 0), pipeline_mode=pl.Buffered(1))


def _merge(pm2, y_ret, y_diff, y_dil, x2, wr, wd, wl, wo, l, tm=512):
    T, D = x2.shape
    row = lambda w: pl.BlockSpec((tm, w), lambda i: (i, 0))
    return pl.pallas_call(
        _merge_kernel,
        grid=(T // tm,),
        in_specs=[row(_GATE_W), row(D), row(D), row(y_dil.shape[1]), row(D),
                  _layer_spec(wr, l), _layer_spec(wd, l), _layer_spec(wl, l), _layer_spec(wo, l)],
        out_specs=row(D),
        out_shape=jax.ShapeDtypeStruct((T, D), F32),
        compiler_params=_cparams(("parallel",)),
        name="merge_out",
    )(pm2, y_ret, y_diff, y_dil, x2, wr, wd, wl, wo)


def _ffn_kernel(x_ref, g_ref, wg_ref, wu_ref, wd_ref, gf_ref, o_ref, *, final):
    x = x_ref[...]
    h = _rms(x, g_ref[...], RMS_EPS).astype(BF16)
    a = _dot(h, wg_ref[...])
    u = _dot(h, wu_ref[...])
    z = (a * jax.nn.sigmoid(a) * u).astype(BF16)
    y = x + _dot(z, wd_ref[...])
    if final:
        y = _rms(y, gf_ref[...], RMS_EPS)
    o_ref[...] = y


def _ffn(x2, g, wg, wu, wd, g_final, l, final, tm=1024):
    T, D = x2.shape
    row = pl.BlockSpec((tm, D), lambda i: (i, 0))
    return pl.pallas_call(
        functools.partial(_ffn_kernel, final=final),
        grid=(T // tm,),
        in_specs=[row, _layer_spec(g, l), _layer_spec(wg, l), _layer_spec(wu, l),
                  _layer_spec(wd, l), _layer_spec(g_final, 0)],
        out_specs=row,
        out_shape=jax.ShapeDtypeStruct((T, D), F32),
        compiler_params=_cparams(("parallel",)),
        name="ffn_final" if final else "ffn",
    )(x2, g, wg, wu, wd, g_final)


def kernel(x, w_in, w_branch_ret, w_branch_diff, w_branch_dil, w_out, norm_mix, norm_ffn, ret_gn_gain, diff_lambda, diff_subln_gain, rel_bias, w_ffn_gate, w_ffn_up, w_ffn_down, norm_final):
    B, S, D = x.shape
    T = B * S
    tiles, far = _bias_tiles(rel_bias)
    ret_tables = _ret_tables(S)
    perm = _residue_perm()
    perm_fwd, perm_back = jnp.asarray(perm, BF16), jnp.asarray(perm.T, BF16)
    x2 = x.reshape(T, D)
    w_branch_ret, w_branch_diff, w_branch_dil, w_out, w_ffn_gate, w_ffn_up, w_ffn_down = (
        w.astype(BF16) for w in (w_branch_ret, w_branch_diff, w_branch_dil, w_out,
                                 w_ffn_gate, w_ffn_up, w_ffn_down))
    norm_mix, norm_ffn = norm_mix.reshape(DEPTH, 1, D), norm_ffn.reshape(DEPTH, 1, D)
    norm_final = norm_final.reshape(1, 1, D)
    for l in range(DEPTH):
        pm2 = _inproj(x2, norm_mix, perm_fwd, w_in, l, B)
        pm = pm2.reshape(B, S, -1)

        y_ret = _retention(pm, ret_tables, ret_gn_gain[l].reshape(1, -1))
        lam_init = 0.8 - 0.6 * math.exp(-0.3 * l)
        y_diff = _diff_attention(pm, tiles, far, diff_lambda[l], diff_subln_gain[l].reshape(1, -1),
                                 lam_init)
        y_dil = _dilated(pm, tiles, perm_back)

        x2 = _merge(pm2, y_ret.reshape(T, -1), y_diff.reshape(T, -1), y_dil.reshape(T, -1), x2,
                    w_branch_ret, w_branch_diff, w_branch_dil, w_out, l)
        x2 = _ffn(x2, norm_ffn, w_ffn_gate, w_ffn_up, w_ffn_down, norm_final, l,
                  final=(l == DEPTH - 1))
    return x2.reshape(B, S, D)
```
